```python
import math
import jax, jax.numpy as jnp
from jax import lax
import numpy as np

D_MODEL = 1024
BATCH = 4
SEQ = 8192
DEPTH = 2
DEC_BATCH = 32
DEC_SEQ = 64
PAST_LEN = 2048

CHUNK = 64
N_MIXERS = 2
N_SGU_LAYERS = (DEPTH + 1) // 2
N_GDN_LAYERS = DEPTH // 2
SGU_CHUNK = 128
D_SGU = 2 * D_MODEL
SGU_GROUPS = 8
SGU_GROUP_DIM = D_SGU // SGU_GROUPS
GDN_HEAD_DIM = 128
GDN_K_HEADS = D_MODEL // 128
GDN_V_HEADS = 2 * GDN_K_HEADS
GDN_D_QK = GDN_K_HEADS * GDN_HEAD_DIM
GDN_D_V = GDN_V_HEADS * GDN_HEAD_DIM
GDN_QKV = 2 * GDN_D_QK + GDN_D_V
GDN_IN = GDN_QKV + GDN_D_V + 2 * GDN_V_HEADS
CONV_K = 4
GDN_CHUNK = CHUNK
D_FF = ((8 * D_MODEL // 3 + 255) // 256) * 256
EPS = 1e-6

kernel_name = "hybrid_stream_sgu_gdn_step"


def rmsnorm(x, w):
    xf = x.astype(jnp.float32)
    return (xf * lax.rsqrt(jnp.mean(xf * xf, axis=-1, keepdims=True) + EPS) * w).astype(x.dtype)


def layernorm(x, g, b):
    xf = x.astype(jnp.float32)
    mu = jnp.mean(xf, axis=-1, keepdims=True)
    var = jnp.mean(jnp.square(xf - mu), axis=-1, keepdims=True)
    return ((xf - mu) * lax.rsqrt(var + 1e-5) * g + b).astype(x.dtype)


def l2norm(x):
    xf = x.astype(jnp.float32)
    return xf * lax.rsqrt(jnp.sum(xf * xf, axis=-1, keepdims=True) + EPS)


def swiglu(h, w_gate, w_up, w_down):
    return (jax.nn.silu(h @ w_gate) * (h @ w_up)) @ w_down


def sgu_mixer(h, w_in, ln_g, ln_b, w_s, b_s, w_out):
    bsz, t_len, _ = h.shape
    uv = jax.nn.gelu(h @ w_in)
    u, v = uv[..., :D_SGU], uv[..., D_SGU:]
    v = layernorm(v, ln_g, ln_b)
    blk = min(SGU_CHUNK, t_len)
    n_blk = t_len // blk
    causal = jnp.tril(jnp.ones((blk, blk), dtype=bool))
    ws = jnp.where(causal, w_s[:, :blk, :blk], 0.0).astype(v.dtype)
    vg = v.reshape(bsz, n_blk, blk, SGU_GROUPS, SGU_GROUP_DIM)
    mixed = jnp.einsum('gts,bnsgc->bntgc', ws, vg) + b_s[:, :blk].T[None, None, :, :, None]
    out = (u * mixed.reshape(bsz, t_len, D_SGU)) @ w_out
    return out, v


def causal_conv(x, buf, w):
    t_len = x.shape[1]
    xp = jnp.concatenate([buf.astype(x.dtype), x], axis=1)
    y = xp[:, 0:t_len] * w[0]
    for i in range(1, CONV_K):
        y = y + xp[:, i:i + t_len] * w[i]
    return y, xp[:, -(CONV_K - 1):]


def gdn_core(q, k, v, beta, log_a, s0):
    bsz, t_len, n_h, _ = q.shape
    dv = v.shape[-1]
    blk = min(GDN_CHUNK, t_len)
    n_blk = t_len // blk
    f32 = jnp.float32

    def to_blocks(a):
        return a.astype(f32).reshape(bsz, n_blk, blk, n_h, -1).transpose(1, 0, 3, 2, 4)

    qb, kb, vb = to_blocks(q), to_blocks(k), to_blocks(v)
    bb = to_blocks(beta[..., None])[..., 0]
    gcum = jnp.cumsum(to_blocks(log_a[..., None])[..., 0], axis=-1)
    incl = jnp.tril(jnp.ones((blk, blk), dtype=bool))
    strict = jnp.tril(jnp.ones((blk, blk), dtype=bool), -1)
    diff = gcum[..., :, None] - gcum[..., None, :]
    decay = jnp.where(incl, jnp.exp(jnp.where(incl, diff, 0.0)), 0.0)
    a_mat = jnp.where(strict, bb[..., :, None] * jnp.einsum('nbhtk,nbhsk->nbhts', kb, kb) * decay, 0.0)
    gam = jnp.exp(gcum)
    rhs = jnp.concatenate([bb[..., None] * vb, (bb * gam)[..., None] * kb], axis=-1)
    sol = lax.linalg.triangular_solve(a_mat + jnp.eye(blk, dtype=f32), rhs,
                                      left_side=True, lower=True, unit_diagonal=True)
    u_b, w_b = sol[..., :dv], sol[..., dv:]
    p_mat = jnp.einsum('nbhtk,nbhsk->nbhts', qb, kb) * decay
    q_dec = gam[..., None] * qb
    k_dec = jnp.exp(gcum[..., -1:] - gcum)[..., None] * kb
    g_last = jnp.exp(gcum[..., -1])

    def step(s, xs):
        u, w, p, qd, kd, gl = xs
        delta = u - jnp.einsum('bhlk,bhkv->bhlv', w, s)
        o = jnp.einsum('bhlk,bhkv->bhlv', qd, s) + jnp.einsum('bhts,bhsv->bhtv', p, delta)
        s = gl[..., None, None] * s + jnp.einsum('bhlk,bhlv->bhkv', kd, delta)
        return s, o

    s_final, o = lax.scan(step, s0.astype(f32), (u_b, w_b, p_mat, q_dec, k_dec, g_last))
    o = o.transpose(1, 0, 3, 2, 4).reshape(bsz, t_len, n_h, dv)
    return o, s_final


def gdn_mixer(h, conv_buf, s0, w_in, w_conv, a_log, dt_bias, w_onorm, w_out):
    bsz, t_len, _ = h.shape
    proj = h @ w_in
    qkv = proj[..., :GDN_QKV]
    z = proj[..., GDN_QKV:GDN_QKV + GDN_D_V]
    b_logit = proj[..., GDN_QKV + GDN_D_V:GDN_QKV + GDN_D_V + GDN_V_HEADS]
    a_logit = proj[..., GDN_QKV + GDN_D_V + GDN_V_HEADS:]
    qkv, new_buf = causal_conv(qkv, conv_buf, w_conv)
    qkv = jax.nn.silu(qkv)
    q = l2norm(qkv[..., :GDN_D_QK].reshape(bsz, t_len, GDN_K_HEADS, GDN_HEAD_DIM)) * (GDN_HEAD_DIM ** -0.5)
    k = l2norm(qkv[..., GDN_D_QK:2 * GDN_D_QK].reshape(bsz, t_len, GDN_K_HEADS, GDN_HEAD_DIM))
    rep = GDN_V_HEADS // GDN_K_HEADS
    q = jnp.repeat(q, rep, axis=2)
    k = jnp.repeat(k, rep, axis=2)
    v = qkv[..., 2 * GDN_D_QK:].reshape(bsz, t_len, GDN_V_HEADS, GDN_HEAD_DIM)
    beta = jax.nn.sigmoid(b_logit.astype(jnp.float32))
    log_a = -jnp.exp(a_log.astype(jnp.float32)) * jax.nn.softplus(
        a_logit.astype(jnp.float32) + dt_bias.astype(jnp.float32))
    o, s_new = gdn_core(q, k, v, beta, log_a, s0)
    o = rmsnorm(o.astype(h.dtype), w_onorm) * jax.nn.silu(z.reshape(bsz, t_len, GDN_V_HEADS, GDN_HEAD_DIM))
    return o.reshape(bsz, t_len, GDN_D_V) @ w_out, new_buf, s_new


def setup_inputs(seed: int = 0) -> dict:
    key = jax.random.key(seed)
    ks = jax.random.split(key, 24)
    f32 = jnp.float32
    nrm = lambda k, shape, scale: jax.random.normal(k, shape, f32) * scale
    dt = jnp.exp(jax.random.uniform(ks[16], (N_GDN_LAYERS, GDN_V_HEADS), f32,
                                    minval=math.log(1e-3), maxval=math.log(1e-1)))
    return {
        "x_prompt": nrm(ks[0], (BATCH, SEQ, D_MODEL), 1.0),
        "x_sample": nrm(ks[1], (DEC_BATCH, DEC_SEQ, D_MODEL), 1.0),
        "state_gdn": nrm(ks[2], (N_GDN_LAYERS, DEC_BATCH, GDN_V_HEADS, GDN_HEAD_DIM, GDN_HEAD_DIM), 0.1),
        "state_conv": nrm(ks[3], (N_GDN_LAYERS, DEC_BATCH, CONV_K - 1, GDN_QKV), 1.0),
        "norm_mix": 1.0 + nrm(ks[4], (DEPTH, D_MODEL), 0.02),
        "norm_ffn": 1.0 + nrm(ks[5], (DEPTH, D_MODEL), 0.02),
        "norm_final": 1.0 + nrm(ks[6], (D_MODEL,), 0.02),
        "sgu_w_in": nrm(ks[7], (N_SGU_LAYERS, D_MODEL, 2 * D_SGU), D_MODEL ** -0.5),
        "sgu_ln_g": 1.0 + nrm(ks[8], (N_SGU_LAYERS, D_SGU), 0.02),
        "sgu_ln_b": nrm(ks[9], (N_SGU_LAYERS, D_SGU), 0.02),
        "sgu_w_s": nrm(ks[10], (N_SGU_LAYERS, SGU_GROUPS, SGU_CHUNK, SGU_CHUNK), SGU_CHUNK ** -0.5),
        "sgu_b_s": 1.0 + nrm(ks[11], (N_SGU_LAYERS, SGU_GROUPS, SGU_CHUNK), 0.02),
        "sgu_w_out": nrm(ks[12], (N_SGU_LAYERS, D_SGU, D_MODEL), D_SGU ** -0.5),
        "gdn_w_in": nrm(ks[13], (N_GDN_LAYERS, D_MODEL, GDN_IN), D_MODEL ** -0.5),
        "gdn_w_conv": nrm(ks[14], (N_GDN_LAYERS, CONV_K, GDN_QKV), CONV_K ** -0.5),
        "gdn_a_log": jnp.log(jax.random.uniform(ks[15], (N_GDN_LAYERS, GDN_V_HEADS), f32, minval=1.0, maxval=16.0)),
        "gdn_dt_bias": dt + jnp.log(-jnp.expm1(-dt)),
        "gdn_w_onorm": 1.0 + nrm(ks[17], (N_GDN_LAYERS, GDN_HEAD_DIM), 0.02),
        "gdn_w_out": nrm(ks[18], (N_GDN_LAYERS, GDN_D_V, D_MODEL), GDN_D_V ** -0.5),
        "ffn_w_gate": nrm(ks[19], (DEPTH, D_MODEL, D_FF), D_MODEL ** -0.5),
        "ffn_w_up": nrm(ks[20], (DEPTH, D_MODEL, D_FF), D_MODEL ** -0.5),
        "ffn_w_down": nrm(ks[21], (DEPTH, D_FF, D_MODEL), D_FF ** -0.5),
    }


def reference(x_prompt, x_sample, state_gdn, state_conv, norm_mix, norm_ffn, norm_final,
              sgu_w_in, sgu_ln_g, sgu_ln_b, sgu_w_s, sgu_b_s, sgu_w_out,
              gdn_w_in, gdn_w_conv, gdn_a_log, gdn_dt_bias, gdn_w_onorm, gdn_w_out,
              ffn_w_gate, ffn_w_up, ffn_w_down):
    xp, xs = x_prompt, x_sample
    gdn_s_p, gdn_c_p, gdn_s_s, gdn_c_s, sgu_v_s = [], [], [], [], []
    for i in range(DEPTH):
        j = i // N_MIXERS
        hp = rmsnorm(xp, norm_mix[i])
        hs = rmsnorm(xs, norm_mix[i])
        if i % N_MIXERS == 0:
            sgu_args = (sgu_w_in[j], sgu_ln_g[j], sgu_ln_b[j], sgu_w_s[j], sgu_b_s[j], sgu_w_out[j])
            mp, _ = sgu_mixer(hp, *sgu_args)
            ms, v_rows = sgu_mixer(hs, *sgu_args)
            sgu_v_s.append(v_rows)
        else:
            gdn_args = (gdn_w_in[j], gdn_w_conv[j], gdn_a_log[j], gdn_dt_bias[j], gdn_w_onorm[j], gdn_w_out[j])
            zero_buf = jnp.zeros((xp.shape[0], CONV_K - 1, GDN_QKV), xp.dtype)
            zero_s = jnp.zeros((xp.shape[0], GDN_V_HEADS, GDN_HEAD_DIM, GDN_HEAD_DIM), jnp.float32)
            mp, cp, sp = gdn_mixer(hp, zero_buf, zero_s, *gdn_args)
            ms, cs, ss = gdn_mixer(hs, state_conv[j], state_gdn[j], *gdn_args)
            gdn_s_p.append(sp)
            gdn_c_p.append(cp)
            gdn_s_s.append(ss)
            gdn_c_s.append(cs)
        xp = xp + mp
        xs = xs + ms
        xp = xp + swiglu(rmsnorm(xp, norm_ffn[i]), ffn_w_gate[i], ffn_w_up[i], ffn_w_down[i])
        xs = xs + swiglu(rmsnorm(xs, norm_ffn[i]), ffn_w_gate[i], ffn_w_up[i], ffn_w_down[i])
    y_prompt = rmsnorm(xp, norm_final)
    y_sample = rmsnorm(xs, norm_final)
    new_state_gdn_prompt = jnp.stack(gdn_s_p)
    new_state_conv_prompt = jnp.stack(gdn_c_p)
    new_state_gdn_sample = jnp.stack(gdn_s_s)
    new_state_conv_sample = jnp.stack(gdn_c_s)
    new_state_sgu_v_sample = jnp.stack(sgu_v_s)
    return (y_prompt, y_sample, new_state_gdn_prompt, new_state_conv_prompt,
            new_state_gdn_sample, new_state_conv_sample, new_state_sgu_v_sample)
```

```python
import functools
import math

import jax
import jax.numpy as jnp
from jax import lax
from jax.experimental import pallas as pl
from jax.experimental.pallas import tpu as pltpu

D_MODEL = 1024
D_SGU = 2048
SGU_GROUPS = 8
SGU_GROUP_DIM = D_SGU // SGU_GROUPS
SGU_CHUNK = 128
HEAD_DIM = 128
K_HEADS = 8
V_HEADS = 16
D_QK = K_HEADS * HEAD_DIM
D_V = V_HEADS * HEAD_DIM
D_QKV = 2 * D_QK + D_V
CONV_K = 4
GDN_CHUNK = 64
D_FF = 2816
EPS = 1e-6
LN_EPS = 1e-5

VMEM_LIMIT_BYTES = 56 * 1024 * 1024
SUBLANES = 8
COL_BLOCK = 512
FFN_COL_BLOCK = 256

F32 = jnp.float32
BF16 = jnp.bfloat16
HIGHEST = lax.Precision.HIGHEST


def _const_spec(shape):
    nd = len(shape)
    return pl.BlockSpec(shape, lambda *_: (0,) * nd, pipeline_mode=pl.Buffered(1))


def _params(n_grid):
    return pltpu.CompilerParams(dimension_semantics=("arbitrary",) * n_grid,
                                vmem_limit_bytes=VMEM_LIMIT_BYTES)


def _rms_scale(x):
    return lax.rsqrt(jnp.mean(x * x, axis=-1, keepdims=True) + EPS)


def _silu(x):
    return x * jax.nn.sigmoid(x)


def _gelu_tanh(x):
    return 0.5 * x * (1.0 + jnp.tanh(math.sqrt(2.0 / math.pi) * (x + 0.044715 * (x * x * x))))


def _softplus(x):
    return jnp.maximum(x, 0.0) + jnp.log1p(jnp.exp(-jnp.abs(x)))


def _dot(a, b):
    return jnp.dot(a, b, preferred_element_type=F32)


def _dot_nt(a, b, precision=None):
    return lax.dot_general(a, b, (((1,), (1,)), ((), ())), precision=precision,
                           preferred_element_type=F32)


def _dot_tn(a, b):
    return lax.dot_general(a, b, (((0,), (0,)), ((), ())), preferred_element_type=F32)


def _ffn_store(x, nw_ref, wg_ref, wu_ref, wd_ref, hb_ref, act_ref, o_ref, nf_ref):
    hb_ref[...] = (x * _rms_scale(x) * nw_ref[...]).astype(BF16)
    for c in range(0, D_FF, FFN_COL_BLOCK):
        g = _dot(hb_ref[...], wg_ref[:, c:c + FFN_COL_BLOCK])
        u = _dot(hb_ref[...], wu_ref[:, c:c + FFN_COL_BLOCK])
        act_ref[:, c:c + FFN_COL_BLOCK] = (_silu(g) * u).astype(BF16)
    y = x + _dot(act_ref[...], wd_ref[...])
    if nf_ref is not None:
        y = y * _rms_scale(y) * nf_ref[...]
    o_ref[...] = y


def _sgu_kernel(x_ref, nw_ref, win_ref, lng_ref, lnb_ref, ws_ref, bs_ref, wout_ref,
                *rest, chunk, write_v):
    if write_v:
        o_ref, v_ref, hb_ref, u_ref, vv_ref, gated_ref = rest
    else:
        o_ref, hb_ref, u_ref, vv_ref, gated_ref = rest
        v_ref = None
    tb = x_ref.shape[0]
    x = x_ref[...]
    hb_ref[...] = (x * _rms_scale(x) * nw_ref[...]).astype(BF16)

    vsum = jnp.zeros((tb, 1), F32)
    for c in range(0, 2 * D_SGU, COL_BLOCK):
        uv = _gelu_tanh(_dot(hb_ref[...], win_ref[:, c:c + COL_BLOCK]))
        if c < D_SGU:
            u_ref[:, c:c + COL_BLOCK] = uv
        else:
            vv_ref[:, c - D_SGU:c - D_SGU + COL_BLOCK] = uv
            vsum = vsum + jnp.sum(uv, axis=-1, keepdims=True)
    mu = vsum * (1.0 / D_SGU)
    vvar = jnp.zeros((tb, 1), F32)
    for c in range(0, D_SGU, COL_BLOCK):
        d = vv_ref[:, c:c + COL_BLOCK] - mu
        vvar = vvar + jnp.sum(d * d, axis=-1, keepdims=True)
    rstd = lax.rsqrt(vvar * (1.0 / D_SGU) + LN_EPS)

    row = lax.broadcasted_iota(jnp.int32, (chunk, chunk), 0)
    col = lax.broadcasted_iota(jnp.int32, (chunk, chunk), 1)
    causal = row >= col
    for g in range(SGU_GROUPS):
        c0 = g * SGU_GROUP_DIM
        cs = slice(c0, c0 + SGU_GROUP_DIM)
        vn = (vv_ref[:, cs] - mu) * rstd * lng_ref[:, cs] + lnb_ref[:, cs]
        if write_v:
            v_ref[:, cs] = vn
        vnb = vn.astype(BF16)
        wsg = jnp.where(causal, ws_ref[g], 0.0).astype(BF16)
        bias = bs_ref[:, g:g + 1]
        for r in range(0, tb, chunk):
            mixed = _dot(wsg, vnb[r:r + chunk]) + bias
            gated_ref[r:r + chunk, cs] = (u_ref[r:r + chunk, cs] * mixed).astype(BF16)
    o_ref[...] = x + _dot(gated_ref[...], wout_ref[...])


def _sgu_layer(x, norm_w, w_in, ln_g, ln_b, w_s, b_s, w_out, *, chunk, tb, write_v):
    n = x.shape[0]
    ws = w_s[:, :chunk, :chunk]
    bs_t = b_s[:, :chunk].T
    out_shape = [jax.ShapeDtypeStruct((n, D_MODEL), F32)]
    out_specs = [pl.BlockSpec((tb, D_MODEL), lambda i: (i, 0))]
    if write_v:
        out_shape.append(jax.ShapeDtypeStruct((n, D_SGU), F32))
        out_specs.append(pl.BlockSpec((tb, D_SGU), lambda i: (i, 0)))
    res = pl.pallas_call(
        functools.partial(_sgu_kernel, chunk=chunk, write_v=write_v),
        grid=(n // tb,),
        in_specs=[
            pl.BlockSpec((tb, D_MODEL), lambda i: (i, 0)),
            _const_spec((1, D_MODEL)),
            _const_spec((D_MODEL, 2 * D_SGU)),
            _const_spec((1, D_SGU)),
            _const_spec((1, D_SGU)),
            _const_spec((SGU_GROUPS, chunk, chunk)),
            _const_spec((chunk, SGU_GROUPS)),
            _const_spec((D_SGU, D_MODEL)),
        ],
        out_specs=out_specs,
        out_shape=out_shape,
        scratch_shapes=[
            pltpu.VMEM((tb, D_MODEL), BF16),
            pltpu.VMEM((tb, D_SGU), F32),
            pltpu.VMEM((tb, D_SGU), F32),
            pltpu.VMEM((tb, D_SGU), BF16),
        ],
        compiler_params=_params(1),
        name="sgu_mixer",
    )(x, norm_w.reshape(1, D_MODEL), w_in.astype(BF16), ln_g.reshape(1, D_SGU),
      ln_b.reshape(1, D_SGU), ws, bs_t, w_out.astype(BF16))
    return res if write_v else (res[0], None)


def _ffn_kernel(x_ref, nw_ref, wg_ref, wu_ref, wd_ref, o_ref, hb_ref, act_ref):
    _ffn_store(x_ref[...], nw_ref, wg_ref, wu_ref, wd_ref, hb_ref, act_ref, o_ref, None)


def _ffn_layer(x, norm_w, w_gate, w_up, w_down, *, tb):
    n = x.shape[0]
    return pl.pallas_call(
        _ffn_kernel,
        grid=(n // tb,),
        in_specs=[
            pl.BlockSpec((tb, D_MODEL), lambda i: (i, 0)),
            _const_spec((1, D_MODEL)),
            _const_spec((D_MODEL, D_FF)),
            _const_spec((D_MODEL, D_FF)),
            _const_spec((D_FF, D_MODEL)),
        ],
        out_specs=pl.BlockSpec((tb, D_MODEL), lambda i: (i, 0)),
        out_shape=jax.ShapeDtypeStruct((n, D_MODEL), F32),
        scratch_shapes=[pltpu.VMEM((tb, D_MODEL), BF16), pltpu.VMEM((tb, D_FF), BF16)],
        compiler_params=_params(1),
        name="ffn",
    )(x, norm_w.reshape(1, D_MODEL), w_gate.astype(BF16), w_up.astype(BF16), w_down.astype(BF16))


def _gdn_proj_kernel(x_ref, cbuf_ref, nw_ref, wqkv_ref, wz_ref, wbat_ref, wconv_ref,
                     alog_ref, dtb_ref,
                     q_ref, k_ref, v_ref, z_ref, beta_ref, g_ref, cnew_ref,
                     hb_ref, xp_ref, *, chunk):
    tb = x_ref.shape[1]
    t = pl.program_id(1)
    x = x_ref[0]
    hb_ref[...] = (x * _rms_scale(x) * nw_ref[...]).astype(BF16)

    @pl.when(t == 0)
    def _():
        xp_ref[0:SUBLANES, :] = cbuf_ref[0]

    for c in range(0, D_QKV, COL_BLOCK):
        xp_ref[SUBLANES:SUBLANES + tb, c:c + COL_BLOCK] = _dot(hb_ref[...], wqkv_ref[:, c:c + COL_BLOCK])

    for c in range(0, D_QKV, COL_BLOCK):
        cs = slice(c, c + COL_BLOCK)
        y = xp_ref[SUBLANES - 3:SUBLANES - 3 + tb, cs] * wconv_ref[0:1, cs]
        for i in range(1, CONV_K):
            y = y + xp_ref[SUBLANES - 3 + i:SUBLANES - 3 + i + tb, cs] * wconv_ref[i:i + 1, cs]
        y = _silu(y)
        if c < 2 * D_QK:
            dst, c_dst, scale = (q_ref, c, HEAD_DIM ** -0.5) if c < D_QK else (k_ref, c - D_QK, 1.0)
            for h in range(0, COL_BLOCK, HEAD_DIM):
                yh = y[:, h:h + HEAD_DIM]
                inv = lax.rsqrt(jnp.sum(yh * yh, axis=-1, keepdims=True) + EPS) * scale
                dst[0, :, c_dst + h:c_dst + h + HEAD_DIM] = yh * inv
        else:
            v_ref[0, :, c - 2 * D_QK:c - 2 * D_QK + COL_BLOCK] = y

    tail = xp_ref[tb:tb + SUBLANES, :]
    xp_ref[0:SUBLANES, :] = tail
    cnew_ref[0] = tail

    for c in range(0, D_V, COL_BLOCK):
        z_ref[0, :, c:c + COL_BLOCK] = _dot(hb_ref[...], wz_ref[:, c:c + COL_BLOCK])

    ba_t = _dot_nt(wbat_ref[...], hb_ref[...])
    beta_t = jax.nn.sigmoid(ba_t[:V_HEADS])
    loga_t = -jnp.exp(alog_ref[...]) * _softplus(ba_t[V_HEADS:] + dtb_ref[...])
    row = lax.broadcasted_iota(jnp.int32, (chunk, chunk), 0)
    col = lax.broadcasted_iota(jnp.int32, (chunk, chunk), 1)
    upper = (row <= col).astype(F32)
    for j in range(tb // chunk):
        js = slice(j * chunk, (j + 1) * chunk)
        beta_ref[0, j] = beta_t[:, js]
        g_ref[0, j] = jnp.dot(loga_t[:, js], upper, precision=HIGHEST, preferred_element_type=F32)


def _gdn_proj(x, conv_buf, norm_w, w_in, w_conv, a_log, dt_bias, *, chunk, tb):
    b, t_len, _ = x.shape
    nt = t_len // tb
    npc = tb // chunk
    cbuf = jnp.pad(conv_buf, ((0, 0), (SUBLANES - (CONV_K - 1), 0), (0, 0)))
    w_qkv = w_in[:, :D_QKV].astype(BF16)
    w_z = w_in[:, D_QKV:D_QKV + D_V].astype(BF16)
    w_ba_t = w_in[:, D_QKV + D_V:].T.astype(BF16)
    tok = lambda width: pl.BlockSpec((1, tb, width), lambda i, j: (i, j, 0))
    rows = pl.BlockSpec((1, npc, V_HEADS, chunk), lambda i, j: (i, j, 0, 0))
    return pl.pallas_call(
        functools.partial(_gdn_proj_kernel, chunk=chunk),
        grid=(b, nt),
        in_specs=[
            tok(D_MODEL),
            pl.BlockSpec((1, SUBLANES, D_QKV), lambda i, j: (i, 0, 0)),
            _const_spec((1, D_MODEL)),
            _const_spec((D_MODEL, D_QKV)),
            _const_spec((D_MODEL, D_V)),
            _const_spec((2 * V_HEADS, D_MODEL)),
            _const_spec((CONV_K, D_QKV)),
            _const_spec((V_HEADS, 1)),
            _const_spec((V_HEADS, 1)),
        ],
        out_specs=[tok(D_QK), tok(D_QK), tok(D_V), tok(D_V), rows, rows,
                   pl.BlockSpec((1, SUBLANES, D_QKV), lambda i, j: (i, 0, 0))],
        out_shape=[
            jax.ShapeDtypeStruct((b, t_len, D_QK), F32),
            jax.ShapeDtypeStruct((b, t_len, D_QK), F32),
            jax.ShapeDtypeStruct((b, t_len, D_V), F32),
            jax.ShapeDtypeStruct((b, t_len, D_V), F32),
            jax.ShapeDtypeStruct((b, t_len // chunk, V_HEADS, chunk), F32),
            jax.ShapeDtypeStruct((b, t_len // chunk, V_HEADS, chunk), F32),
            jax.ShapeDtypeStruct((b, SUBLANES, D_QKV), F32),
        ],
        scratch_shapes=[pltpu.VMEM((tb, D_MODEL), BF16),
                        pltpu.VMEM((tb + SUBLANES, D_QKV), F32)],
        compiler_params=_params(2),
        name="gdn_proj",
    )(x, cbuf, norm_w.reshape(1, D_MODEL), w_qkv, w_z, w_ba_t, w_conv,
      a_log.reshape(V_HEADS, 1), dt_bias.reshape(V_HEADS, 1))


def _gdn_core_kernel(q_ref, k_ref, v_ref, beta_ref, g_ref, s0_ref, won_ref, o_ref, s_ref, *, chunk):
    t = pl.program_id(2)

    @pl.when(t == 0)
    def _():
        s_ref[...] = s0_ref[...]

    n_chunks = q_ref.shape[1] // chunk
    row = lax.broadcasted_iota(jnp.int32, (chunk, chunk), 0)
    col = lax.broadcasted_iota(jnp.int32, (chunk, chunk), 1)
    incl = row >= col
    strict = row > col
    eye = (row == col).astype(F32)
    n_double = chunk.bit_length() - 2

    for c in range(n_chunks):
        rs = slice(c * chunk, (c + 1) * chunk)
        qb = q_ref[0, rs, :].astype(BF16)
        kb = k_ref[0, rs, :].astype(BF16)
        kk = _dot_nt(kb, kb)
        qk = _dot_nt(qb, kb)
        s_pair = jnp.concatenate([s_ref[0, 0], s_ref[0, 1]], axis=1).astype(BF16)
        kq_s = _dot(jnp.concatenate([kb, qb], axis=0), s_pair)
        rows_t = jnp.concatenate([beta_ref[0, c, 0], g_ref[0, c, 0],
                                  jnp.zeros((SUBLANES - 4, chunk), F32)], axis=0)
        cols = _dot_nt(eye, rows_t, precision=HIGHEST)
        for e in range(2):
            vs = slice(e * HEAD_DIM, (e + 1) * HEAD_DIM)
            g_row = g_ref[0, c, 0, e:e + 1, :]
            b_col = cols[:, e:e + 1]
            g_col = cols[:, 2 + e:3 + e]
            g_last = g_row[:, chunk - 1:chunk]
            decay = jnp.where(incl, jnp.exp(jnp.where(incl, g_col - g_row, 0.0)), 0.0)
            x_mat = jnp.where(strict, -(b_col * kk * decay), 0.0)
            t_mat = eye + x_mat
            p_mat = x_mat
            for _ in range(n_double):
                p_mat = jnp.dot(p_mat, p_mat, precision=HIGHEST, preferred_element_type=F32)
                t_mat = t_mat + jnp.dot(t_mat, p_mat, precision=HIGHEST, preferred_element_type=F32)
            gam = jnp.exp(g_col)
            rhs = b_col * (v_ref[0, rs, vs] - gam * kq_s[:chunk, vs])
            delta = _dot(t_mat.astype(BF16), rhs.astype(BF16))
            p_attn = jnp.where(incl, qk * decay, 0.0)
            o = gam * kq_s[chunk:, vs] + _dot(p_attn.astype(BF16), delta.astype(BF16))
            o_ref[0, rs, vs] = o * _rms_scale(o) * won_ref[...]
            k_dec = jnp.exp(g_last - g_col)
            s_ref[0, e] = jnp.exp(g_last) * s_ref[0, e] + _dot_tn(kb, (k_dec * delta).astype(BF16))


def _gdn_core(q, k, v, beta_t, g_t, s0, w_onorm, *, chunk, tb):
    b, t_len, _ = q.shape
    nt = t_len // tb
    npc = tb // chunk
    pair_rows = lambda a: a.reshape(b, t_len // chunk, K_HEADS, 2, chunk)
    rows = pl.BlockSpec((1, npc, 1, 2, chunk), lambda i, h, j: (i, j, h, 0, 0))
    state = pl.BlockSpec((1, 2, HEAD_DIM, HEAD_DIM), lambda i, h, j: (i, h, 0, 0))
    return pl.pallas_call(
        functools.partial(_gdn_core_kernel, chunk=chunk),
        grid=(b, K_HEADS, nt),
        in_specs=[
            pl.BlockSpec((1, tb, HEAD_DIM), lambda i, h, j: (i, j, h)),
            pl.BlockSpec((1, tb, HEAD_DIM), lambda i, h, j: (i, j, h)),
            pl.BlockSpec((1, tb, 2 * HEAD_DIM), lambda i, h, j: (i, j, h)),
            rows, rows, state,
            _const_spec((1, HEAD_DIM)),
        ],
        out_specs=[pl.BlockSpec((1, tb, 2 * HEAD_DIM), lambda i, h, j: (i, j, h)), state],
        out_shape=[jax.ShapeDtypeStruct((b, t_len, D_V), F32),
                   jax.ShapeDtypeStruct((b, V_HEADS, HEAD_DIM, HEAD_DIM), F32)],
        compiler_params=_params(3),
        name="gdn_core",
    )(q, k, v, pair_rows(beta_t), pair_rows(g_t), s0, w_onorm.reshape(1, HEAD_DIM))


def _gdn_tail_kernel(o_ref, z_ref, x_ref, wout_ref, nw_ref, wg_ref, wu_ref, wd_ref, nf_ref,
                     y_ref, gated_ref, hb_ref, act_ref):
    for c in range(0, D_V, COL_BLOCK):
        cs = slice(c, c + COL_BLOCK)
        gated_ref[:, cs] = (o_ref[:, cs] * _silu(z_ref[:, cs])).astype(BF16)
    x = x_ref[...] + _dot(gated_ref[...], wout_ref[...])
    _ffn_store(x, nw_ref, wg_ref, wu_ref, wd_ref, hb_ref, act_ref, y_ref, nf_ref)


def _gdn_tail(o, z, x, w_out, norm_w, w_gate, w_up, w_down, norm_final, *, tb):
    n = x.shape[0]
    return pl.pallas_call(
        _gdn_tail_kernel,
        grid=(n // tb,),
        in_specs=[
            pl.BlockSpec((tb, D_V), lambda i: (i, 0)),
            pl.BlockSpec((tb, D_V), lambda i: (i, 0)),
            pl.BlockSpec((tb, D_MODEL), lambda i: (i, 0)),
            _const_spec((D_V, D_MODEL)),
            _const_spec((1, D_MODEL)),
            _const_spec((D_MODEL, D_FF)),
            _const_spec((D_MODEL, D_FF)),
            _const_spec((D_FF, D_MODEL)),
            _const_spec((1, D_MODEL)),
        ],
        out_specs=pl.BlockSpec((tb, D_MODEL), lambda i: (i, 0)),
        out_shape=jax.ShapeDtypeStruct((n, D_MODEL), F32),
        scratch_shapes=[pltpu.VMEM((tb, D_V), BF16), pltpu.VMEM((tb, D_MODEL), BF16),
                        pltpu.VMEM((tb, D_FF), BF16)],
        compiler_params=_params(1),
        name="gdn_out_ffn",
    )(o, z, x, w_out.astype(BF16), norm_w.reshape(1, D_MODEL), w_gate.astype(BF16),
      w_up.astype(BF16), w_down.astype(BF16), norm_final.reshape(1, D_MODEL))


def _trunk(x, conv_buf, s0, p, *, sgu_chunk, gdn_chunk, tb, gdn_tb, core_tb, write_v):
    b, t_len, _ = x.shape
    flat = lambda a: a.reshape(b * t_len, a.shape[-1])
    x0, v_rows = _sgu_layer(flat(x), p["norm_mix"][0], p["sgu_w_in"][0], p["sgu_ln_g"][0],
                            p["sgu_ln_b"][0], p["sgu_w_s"][0], p["sgu_b_s"][0], p["sgu_w_out"][0],
                            chunk=sgu_chunk, tb=tb, write_v=write_v)
    x1 = _ffn_layer(x0, p["norm_ffn"][0], p["ffn_w_gate"][0], p["ffn_w_up"][0], p["ffn_w_down"][0], tb=tb)
    q, k, v, z, beta_t, g_t, conv_tail = _gdn_proj(
        x1.reshape(b, t_len, D_MODEL), conv_buf, p["norm_mix"][1], p["gdn_w_in"][0],
        p["gdn_w_conv"][0], p["gdn_a_log"][0], p["gdn_dt_bias"][0], chunk=gdn_chunk, tb=gdn_tb)
    o, s_new = _gdn_core(q, k, v, beta_t, g_t, s0, p["gdn_w_onorm"][0], chunk=gdn_chunk, tb=core_tb)
    y = _gdn_tail(flat(o), flat(z), x1, p["gdn_w_out"][0], p["norm_ffn"][1], p["ffn_w_gate"][1],
                  p["ffn_w_up"][1], p["ffn_w_down"][1], p["norm_final"], tb=tb)
    conv_new = conv_tail[:, SUBLANES - (CONV_K - 1):, :]
    if write_v:
        v_rows = v_rows.reshape(b, t_len, D_SGU)
    return y.reshape(b, t_len, D_MODEL), s_new, conv_new, v_rows


def kernel(x_prompt, x_sample, state_gdn, state_conv, norm_mix, norm_ffn, norm_final, sgu_w_in, sgu_ln_g, sgu_ln_b, sgu_w_s, sgu_b_s, sgu_w_out, gdn_w_in, gdn_w_conv, gdn_a_log, gdn_dt_bias, gdn_w_onorm, gdn_w_out, ffn_w_gate, ffn_w_up, ffn_w_down):
    p = dict(norm_mix=norm_mix, norm_ffn=norm_ffn, norm_final=norm_final, sgu_w_in=sgu_w_in,
             sgu_ln_g=sgu_ln_g, sgu_ln_b=sgu_ln_b, sgu_w_s=sgu_w_s, sgu_b_s=sgu_b_s,
             sgu_w_out=sgu_w_out, gdn_w_in=gdn_w_in, gdn_w_conv=gdn_w_conv, gdn_a_log=gdn_a_log,
             gdn_dt_bias=gdn_dt_bias, gdn_w_onorm=gdn_w_onorm, gdn_w_out=gdn_w_out,
             ffn_w_gate=ffn_w_gate, ffn_w_up=ffn_w_up, ffn_w_down=ffn_w_down)
    bp = x_prompt.shape[0]
    dec_seq = x_sample.shape[1]
    zero_conv = jnp.zeros((bp, CONV_K - 1, D_QKV), F32)
    zero_state = jnp.zeros((bp, V_HEADS, HEAD_DIM, HEAD_DIM), F32)
    yp, sp, cp, _ = _trunk(x_prompt, zero_conv, zero_state, p, sgu_chunk=SGU_CHUNK,
                           gdn_chunk=2 * GDN_CHUNK, tb=256, gdn_tb=256, core_tb=512, write_v=False)
    ys, ss, cs, vs = _trunk(x_sample, state_conv[0], state_gdn[0], p, sgu_chunk=dec_seq,
                            gdn_chunk=dec_seq, tb=256, gdn_tb=dec_seq, core_tb=dec_seq, write_v=True)
    return (yp, ys, sp[None], cp[None], ss[None], cs[None], vs[None])
```

```python
import functools
import math

import jax
import jax.numpy as jnp
from jax import lax
from jax.experimental import pallas as pl
from jax.experimental.pallas import tpu as pltpu

D_MODEL = 1024
D_SGU = 2048
SGU_GROUPS = 8
SGU_GROUP_DIM = D_SGU // SGU_GROUPS
SGU_CHUNK = 128
HEAD_DIM = 128
K_HEADS = 8
V_HEADS = 16
D_QK = K_HEADS * HEAD_DIM
D_V = V_HEADS * HEAD_DIM
D_QKV = 2 * D_QK + D_V
CONV_K = 4
GDN_CHUNK = 64
D_FF = 2816
EPS = 1e-6
LN_EPS = 1e-5

VMEM_LIMIT_BYTES = 56 * 1024 * 1024
SUBLANES = 8
COL_BLOCK = 512
FFN_COL_BLOCK = 256
INV_BASE = 8

F32 = jnp.float32
BF16 = jnp.bfloat16
HIGHEST = lax.Precision.HIGHEST


def _const_spec(shape):
    nd = len(shape)
    return pl.BlockSpec(shape, lambda *_: (0,) * nd, pipeline_mode=pl.Buffered(1))


def _params(n_grid):
    return pltpu.CompilerParams(dimension_semantics=("arbitrary",) * n_grid,
                                vmem_limit_bytes=VMEM_LIMIT_BYTES)


def _rms_scale(x):
    return lax.rsqrt(jnp.mean(x * x, axis=-1, keepdims=True) + EPS)


def _silu(x):
    return x * jax.nn.sigmoid(x)


def _gelu_tanh(x):
    return 0.5 * x * (1.0 + jnp.tanh(math.sqrt(2.0 / math.pi) * (x + 0.044715 * (x * x * x))))


def _softplus(x):
    return jnp.maximum(x, 0.0) + jnp.log1p(jnp.exp(-jnp.abs(x)))


def _dot(a, b):
    return jnp.dot(a, b, preferred_element_type=F32)


def _dot_nt(a, b, precision=None):
    return lax.dot_general(a, b, (((1,), (1,)), ((), ())), precision=precision,
                           preferred_element_type=F32)


def _dot_tn(a, b):
    return lax.dot_general(a, b, (((0,), (0,)), ((), ())), preferred_element_type=F32)


def _ffn_store(x, nw_ref, wg_ref, wu_ref, wd_ref, hb_ref, act_ref, o_ref, nf_ref):
    hb_ref[...] = (x * _rms_scale(x) * nw_ref[...]).astype(BF16)
    for c in range(0, D_FF, FFN_COL_BLOCK):
        g = _dot(hb_ref[...], wg_ref[:, c:c + FFN_COL_BLOCK])
        u = _dot(hb_ref[...], wu_ref[:, c:c + FFN_COL_BLOCK])
        act_ref[:, c:c + FFN_COL_BLOCK] = (_silu(g) * u).astype(BF16)
    y = x + _dot(act_ref[...], wd_ref[...])
    if nf_ref is not None:
        y = y * _rms_scale(y) * nf_ref[...]
    o_ref[...] = y


def _sgu_kernel(x_ref, nw_ref, win_ref, lng_ref, lnb_ref, ws_ref, bs_ref, wout_ref,
                *rest, chunk, write_v):
    if write_v:
        o_ref, v_ref, hb_ref, u_ref, vv_ref, gated_ref = rest
    else:
        o_ref, hb_ref, u_ref, vv_ref, gated_ref = rest
        v_ref = None
    tb = x_ref.shape[0]
    x = x_ref[...]
    hb_ref[...] = (x * _rms_scale(x) * nw_ref[...]).astype(BF16)

    vsum = jnp.zeros((tb, 1), F32)
    for c in range(0, 2 * D_SGU, COL_BLOCK):
        uv = _gelu_tanh(_dot(hb_ref[...], win_ref[:, c:c + COL_BLOCK]))
        if c < D_SGU:
            u_ref[:, c:c + COL_BLOCK] = uv
        else:
            vv_ref[:, c - D_SGU:c - D_SGU + COL_BLOCK] = uv
            vsum = vsum + jnp.sum(uv, axis=-1, keepdims=True)
    mu = vsum * (1.0 / D_SGU)
    vvar = jnp.zeros((tb, 1), F32)
    for c in range(0, D_SGU, COL_BLOCK):
        d = vv_ref[:, c:c + COL_BLOCK] - mu
        vvar = vvar + jnp.sum(d * d, axis=-1, keepdims=True)
    rstd = lax.rsqrt(vvar * (1.0 / D_SGU) + LN_EPS)

    row = lax.broadcasted_iota(jnp.int32, (chunk, chunk), 0)
    col = lax.broadcasted_iota(jnp.int32, (chunk, chunk), 1)
    causal = row >= col
    for g in range(SGU_GROUPS):
        c0 = g * SGU_GROUP_DIM
        cs = slice(c0, c0 + SGU_GROUP_DIM)
        vn = (vv_ref[:, cs] - mu) * rstd * lng_ref[:, cs] + lnb_ref[:, cs]
        if write_v:
            v_ref[:, cs] = vn
        vnb = vn.astype(BF16)
        wsg = jnp.where(causal, ws_ref[g], 0.0).astype(BF16)
        bias = bs_ref[:, g:g + 1]
        for r in range(0, tb, chunk):
            mixed = _dot(wsg, vnb[r:r + chunk]) + bias
            gated_ref[r:r + chunk, cs] = (u_ref[r:r + chunk, cs] * mixed).astype(BF16)
    o_ref[...] = x + _dot(gated_ref[...], wout_ref[...])


def _sgu_layer(x, norm_w, w_in, ln_g, ln_b, w_s, b_s, w_out, *, chunk, tb, write_v):
    n = x.shape[0]
    ws = w_s[:, :chunk, :chunk]
    bs_t = b_s[:, :chunk].T
    out_shape = [jax.ShapeDtypeStruct((n, D_MODEL), F32)]
    out_specs = [pl.BlockSpec((tb, D_MODEL), lambda i: (i, 0))]
    if write_v:
        out_shape.append(jax.ShapeDtypeStruct((n, D_SGU), F32))
        out_specs.append(pl.BlockSpec((tb, D_SGU), lambda i: (i, 0)))
    res = pl.pallas_call(
        functools.partial(_sgu_kernel, chunk=chunk, write_v=write_v),
        grid=(n // tb,),
        in_specs=[
            pl.BlockSpec((tb, D_MODEL), lambda i: (i, 0)),
            _const_spec((1, D_MODEL)),
            _const_spec((D_MODEL, 2 * D_SGU)),
            _const_spec((1, D_SGU)),
            _const_spec((1, D_SGU)),
            _const_spec((SGU_GROUPS, chunk, chunk)),
            _const_spec((chunk, SGU_GROUPS)),
            _const_spec((D_SGU, D_MODEL)),
        ],
        out_specs=out_specs,
        out_shape=out_shape,
        scratch_shapes=[
            pltpu.VMEM((tb, D_MODEL), BF16),
            pltpu.VMEM((tb, D_SGU), F32),
            pltpu.VMEM((tb, D_SGU), F32),
            pltpu.VMEM((tb, D_SGU), BF16),
        ],
        compiler_params=_params(1),
        name="sgu_mixer",
    )(x, norm_w.reshape(1, D_MODEL), w_in.astype(BF16), ln_g.reshape(1, D_SGU),
      ln_b.reshape(1, D_SGU), ws, bs_t, w_out.astype(BF16))
    return res if write_v else (res[0], None)


def _ffn_kernel(x_ref, nw_ref, wg_ref, wu_ref, wd_ref, o_ref, hb_ref, act_ref):
    _ffn_store(x_ref[...], nw_ref, wg_ref, wu_ref, wd_ref, hb_ref, act_ref, o_ref, None)


def _ffn_layer(x, norm_w, w_gate, w_up, w_down, *, tb):
    n = x.shape[0]
    return pl.pallas_call(
        _ffn_kernel,
        grid=(n // tb,),
        in_specs=[
            pl.BlockSpec((tb, D_MODEL), lambda i: (i, 0)),
            _const_spec((1, D_MODEL)),
            _const_spec((D_MODEL, D_FF)),
            _const_spec((D_MODEL, D_FF)),
            _const_spec((D_FF, D_MODEL)),
        ],
        out_specs=pl.BlockSpec((tb, D_MODEL), lambda i: (i, 0)),
        out_shape=jax.ShapeDtypeStruct((n, D_MODEL), F32),
        scratch_shapes=[pltpu.VMEM((tb, D_MODEL), BF16), pltpu.VMEM((tb, D_FF), BF16)],
        compiler_params=_params(1),
        name="ffn",
    )(x, norm_w.reshape(1, D_MODEL), w_gate.astype(BF16), w_up.astype(BF16), w_down.astype(BF16))


def _gdn_proj_kernel(x_ref, cbuf_ref, nw_ref, wqkv_ref, wz_ref, wbat_ref, wconv_ref,
                     alog_ref, dtb_ref,
                     q_ref, k_ref, v_ref, z_ref, beta_ref, g_ref, cnew_ref,
                     hb_ref, xp_ref, *, chunk):
    tb = x_ref.shape[1]
    t = pl.program_id(1)
    x = x_ref[0]
    hb_ref[...] = (x * _rms_scale(x) * nw_ref[...]).astype(BF16)

    @pl.when(t == 0)
    def _():
        xp_ref[0:SUBLANES, :] = cbuf_ref[0]

    for c in range(0, D_QKV, COL_BLOCK):
        xp_ref[SUBLANES:SUBLANES + tb, c:c + COL_BLOCK] = _dot(hb_ref[...], wqkv_ref[:, c:c + COL_BLOCK])

    for c in range(0, D_QKV, COL_BLOCK):
        cs = slice(c, c + COL_BLOCK)
        y = xp_ref[SUBLANES - 3:SUBLANES - 3 + tb, cs] * wconv_ref[0:1, cs]
        for i in range(1, CONV_K):
            y = y + xp_ref[SUBLANES - 3 + i:SUBLANES - 3 + i + tb, cs] * wconv_ref[i:i + 1, cs]
        y = _silu(y)
        if c < 2 * D_QK:
            dst, c_dst, scale = (q_ref, c, HEAD_DIM ** -0.5) if c < D_QK else (k_ref, c - D_QK, 1.0)
            for h in range(0, COL_BLOCK, HEAD_DIM):
                yh = y[:, h:h + HEAD_DIM]
                inv = lax.rsqrt(jnp.sum(yh * yh, axis=-1, keepdims=True) + EPS) * scale
                dst[0, :, c_dst + h:c_dst + h + HEAD_DIM] = (yh * inv).astype(BF16)
        else:
            v_ref[0, :, c - 2 * D_QK:c - 2 * D_QK + COL_BLOCK] = y

    tail = xp_ref[tb:tb + SUBLANES, :]
    xp_ref[0:SUBLANES, :] = tail
    cnew_ref[0] = tail

    for c in range(0, D_V, COL_BLOCK):
        z_ref[0, :, c:c + COL_BLOCK] = _dot(hb_ref[...], wz_ref[:, c:c + COL_BLOCK])

    ba_t = _dot_nt(wbat_ref[...], hb_ref[...])
    beta_t = jax.nn.sigmoid(ba_t[:V_HEADS])
    loga_t = -jnp.exp(alog_ref[...]) * _softplus(ba_t[V_HEADS:] + dtb_ref[...])
    row = lax.broadcasted_iota(jnp.int32, (chunk, chunk), 0)
    col = lax.broadcasted_iota(jnp.int32, (chunk, chunk), 1)
    upper = (row <= col).astype(F32)
    for j in range(tb // chunk):
        js = slice(j * chunk, (j + 1) * chunk)
        beta_ref[0, j] = beta_t[:, js]
        g_ref[0, j] = jnp.dot(loga_t[:, js], upper, precision=HIGHEST, preferred_element_type=F32)


def _gdn_proj(x, conv_buf, norm_w, w_in, w_conv, a_log, dt_bias, *, chunk, tb):
    b, t_len, _ = x.shape
    nt = t_len // tb
    npc = tb // chunk
    cbuf = jnp.pad(conv_buf, ((0, 0), (SUBLANES - (CONV_K - 1), 0), (0, 0)))
    w_qkv = w_in[:, :D_QKV].astype(BF16)
    w_z = w_in[:, D_QKV:D_QKV + D_V].astype(BF16)
    w_ba_t = w_in[:, D_QKV + D_V:].T.astype(BF16)
    tok = lambda width: pl.BlockSpec((1, tb, width), lambda i, j: (i, j, 0))
    rows = pl.BlockSpec((1, npc, V_HEADS, chunk), lambda i, j: (i, j, 0, 0))
    return pl.pallas_call(
        functools.partial(_gdn_proj_kernel, chunk=chunk),
        grid=(b, nt),
        in_specs=[
            tok(D_MODEL),
            pl.BlockSpec((1, SUBLANES, D_QKV), lambda i, j: (i, 0, 0)),
            _const_spec((1, D_MODEL)),
            _const_spec((D_MODEL, D_QKV)),
            _const_spec((D_MODEL, D_V)),
            _const_spec((2 * V_HEADS, D_MODEL)),
            _const_spec((CONV_K, D_QKV)),
            _const_spec((V_HEADS, 1)),
            _const_spec((V_HEADS, 1)),
        ],
        out_specs=[tok(D_QK), tok(D_QK), tok(D_V), tok(D_V), rows, rows,
                   pl.BlockSpec((1, SUBLANES, D_QKV), lambda i, j: (i, 0, 0))],
        out_shape=[
            jax.ShapeDtypeStruct((b, t_len, D_QK), BF16),
            jax.ShapeDtypeStruct((b, t_len, D_QK), BF16),
            jax.ShapeDtypeStruct((b, t_len, D_V), F32),
            jax.ShapeDtypeStruct((b, t_len, D_V), F32),
            jax.ShapeDtypeStruct((b, t_len // chunk, V_HEADS, chunk), F32),
            jax.ShapeDtypeStruct((b, t_len // chunk, V_HEADS, chunk), F32),
            jax.ShapeDtypeStruct((b, SUBLANES, D_QKV), F32),
        ],
        scratch_shapes=[pltpu.VMEM((tb, D_MODEL), BF16),
                        pltpu.VMEM((tb + SUBLANES, D_QKV), F32)],
        compiler_params=_params(2),
        name="gdn_proj",
    )(x, cbuf, norm_w.reshape(1, D_MODEL), w_qkv, w_z, w_ba_t, w_conv,
      a_log.reshape(V_HEADS, 1), dt_bias.reshape(V_HEADS, 1))


def _gdn_core_kernel(q_ref, k_ref, v_ref, beta_ref, g_ref, s0_ref, won_ref, o_ref, s_ref,
                     t_ref, p_ref, x_ref, attn_ref, kdt_ref, col_ref, ks_ref, dl_ref, *, chunk):
    @pl.when(pl.program_id(1) == 0)
    def _():
        s_ref[...] = s0_ref[...]

    n_chunks = q_ref.shape[1] // chunk
    row = lax.broadcasted_iota(jnp.int32, (chunk, chunk), 0)
    col = lax.broadcasted_iota(jnp.int32, (chunk, chunk), 1)
    incl = row >= col
    strict = row > col
    eye = (row == col).astype(F32)
    diag_blk = (row // INV_BASE) == (col // INV_BASE)
    base_factors = INV_BASE.bit_length() - 1
    problems =[(c, h) for c in range(n_chunks) for h in range(V_HEADS)]
    slot = lambda c, h: c * V_HEADS + h

    def kq_of(c, pr):
        rs = slice(c * chunk, (c + 1) * chunk)
        hs = slice(pr * HEAD_DIM, (pr + 1) * HEAD_DIM)
        return jnp.concatenate([k_ref[0, rs, hs], q_ref[0, rs, hs]], axis=0)

    for c in range(n_chunks):
        g_rows = g_ref[0, c]
        rows_t = jnp.concatenate([beta_ref[0, c], g_rows, jnp.exp(g_rows)], axis=0)
        col_ref[c] = rows_t.T

    for c in range(n_chunks):
        for pr in range(K_HEADS):
            kq = kq_of(c, pr)
            kq_kt = _dot_nt(kq, kq[:chunk])
            k_t = kq[:chunk].astype(F32).T
            for h in (2 * pr, 2 * pr + 1):
                g_row = g_ref[0, c, h:h + 1, :]
                k_dec = jnp.exp(g_row[:, chunk - 1:chunk] - g_row)
                kdt_ref[slot(c, h)] = (k_t * k_dec).astype(BF16)
                b_col = col_ref[c, :, h:h + 1]
                g_col = col_ref[c, :, V_HEADS + h:V_HEADS + h + 1]
                decay = jnp.where(incl, jnp.exp(jnp.where(incl, g_col - g_row, 0.0)), 0.0)
                x_mat = jnp.where(strict, -(b_col * kq_kt[:chunk] * decay), 0.0)
                x_diag = jnp.where(diag_blk, x_mat, 0.0)
                x_ref[slot(c, h)] = x_mat.astype(BF16)
                t_ref[slot(c, h)] = eye + x_diag
                p_ref[slot(c, h)] = x_diag.astype(BF16)
                attn_ref[slot(c, h)] = jnp.where(incl, kq_kt[chunk:] * decay, 0.0).astype(BF16)

    for c, h in problems:
        xd = p_ref[slot(c, h)]
        p_ref[slot(c, h)] = _dot(xd, xd).astype(BF16)
    for j in range(1, base_factors):
        for c, h in problems:
            pb = p_ref[slot(c, h)]
            tm = t_ref[slot(c, h)]
            if j < base_factors - 1:
                pt = _dot(pb, jnp.concatenate([pb, tm.astype(BF16)], axis=1))
                p_ref[slot(c, h)] = pt[:, :chunk].astype(BF16)
                t_ref[slot(c, h)] = tm + pt[:, chunk:]
            else:
                t_ref[slot(c, h)] = tm + _dot(pb, tm.astype(BF16))
    width = INV_BASE
    while width < chunk:
        off = ((row // (2 * width)) == (col // (2 * width))) & ((row // width) != (col // width))
        for c, h in problems:
            tb16 = t_ref[slot(c, h)].astype(BF16)
            w_mat = _dot(jnp.where(off, x_ref[slot(c, h)], jnp.zeros((), BF16)), tb16)
            t_new = t_ref[slot(c, h)] + _dot(tb16, w_mat.astype(BF16))
            if 2 * width < chunk:
                t_ref[slot(c, h)] = t_new
            else:
                p_ref[slot(c, h)] = t_new.astype(BF16)
        width *= 2

    for c in range(n_chunks):
        rs = slice(c * chunk, (c + 1) * chunk)
        for pr in range(K_HEADS):
            s_pair = jnp.concatenate([s_ref[0, 2 * pr], s_ref[0, 2 * pr + 1]], axis=1).astype(BF16)
            ks_ref[pr] = _dot(kq_of(c, pr), s_pair)
        for h in range(V_HEADS):
            es = slice((h % 2) * HEAD_DIM, (h % 2 + 1) * HEAD_DIM)
            b_col = col_ref[c, :, h:h + 1]
            gam = col_ref[c, :, 2 * V_HEADS + h:2 * V_HEADS + h + 1]
            rhs = b_col * (v_ref[0, rs, h * HEAD_DIM:(h + 1) * HEAD_DIM] - gam * ks_ref[h // 2, :chunk, es])
            dl_ref[h] = _dot(p_ref[slot(c, h)], rhs.astype(BF16)).astype(BF16)
        for h in range(V_HEADS):
            s_decay = jnp.exp(g_ref[0, c, h:h + 1, chunk - 1:chunk])
            s_ref[0, h] = s_decay * s_ref[0, h] + _dot(kdt_ref[slot(c, h)], dl_ref[h])
        for h in range(V_HEADS):
            es = slice((h % 2) * HEAD_DIM, (h % 2 + 1) * HEAD_DIM)
            gam = col_ref[c, :, 2 * V_HEADS + h:2 * V_HEADS + h + 1]
            o = gam * ks_ref[h // 2, chunk:, es] + _dot(attn_ref[slot(c, h)], dl_ref[h])
            o_ref[0, rs, h * HEAD_DIM:(h + 1) * HEAD_DIM] = o * _rms_scale(o) * won_ref[...]


def _gdn_core(q, k, v, beta_t, g_t, s0, w_onorm, *, chunk, tb):
    b, t_len, _ = q.shape
    nt = t_len // tb
    npc = tb // chunk
    tok = lambda width: pl.BlockSpec((1, tb, width), lambda i, j: (i, j, 0))
    rows = pl.BlockSpec((1, npc, V_HEADS, chunk), lambda i, j: (i, j, 0, 0))
    state = pl.BlockSpec((1, V_HEADS, HEAD_DIM, HEAD_DIM), lambda i, j: (i, 0, 0, 0))
    return pl.pallas_call(
        functools.partial(_gdn_core_kernel, chunk=chunk),
        grid=(b, nt),
        in_specs=[tok(D_QK), tok(D_QK), tok(D_V), rows, rows, state, _const_spec((1, HEAD_DIM))],
        out_specs=[tok(D_V), state],
        out_shape=[jax.ShapeDtypeStruct((b, t_len, D_V), F32),
                   jax.ShapeDtypeStruct((b, V_HEADS, HEAD_DIM, HEAD_DIM), F32)],
        scratch_shapes=[pltpu.VMEM((npc * V_HEADS, chunk, chunk), F32),
                        pltpu.VMEM((npc * V_HEADS, chunk, chunk), BF16),
                        pltpu.VMEM((npc * V_HEADS, chunk, chunk), BF16),
                        pltpu.VMEM((npc * V_HEADS, chunk, chunk), BF16),
                        pltpu.VMEM((npc * V_HEADS, HEAD_DIM, chunk), BF16),
                        pltpu.VMEM((npc, chunk, 3 * V_HEADS), F32),
                        pltpu.VMEM((K_HEADS, 2 * chunk, 2 * HEAD_DIM), F32),
                        pltpu.VMEM((V_HEADS, chunk, HEAD_DIM), BF16)],
        compiler_params=_params(2),
        name="gdn_core",
    )(q, k, v, beta_t, g_t, s0, w_onorm.reshape(1, HEAD_DIM))


def _gdn_tail_kernel(o_ref, z_ref, x_ref, wout_ref, nw_ref, wg_ref, wu_ref, wd_ref, nf_ref,
                     y_ref, gated_ref, hb_ref, act_ref):
    for c in range(0, D_V, COL_BLOCK):
        cs = slice(c, c + COL_BLOCK)
        gated_ref[:, cs] = (o_ref[:, cs] * _silu(z_ref[:, cs])).astype(BF16)
    x = x_ref[...] + _dot(gated_ref[...], wout_ref[...])
    _ffn_store(x, nw_ref, wg_ref, wu_ref, wd_ref, hb_ref, act_ref, y_ref, nf_ref)


def _gdn_tail(o, z, x, w_out, norm_w, w_gate, w_up, w_down, norm_final, *, tb):
    n = x.shape[0]
    return pl.pallas_call(
        _gdn_tail_kernel,
        grid=(n // tb,),
        in_specs=[
            pl.BlockSpec((tb, D_V), lambda i: (i, 0)),
            pl.BlockSpec((tb, D_V), lambda i: (i, 0)),
            pl.BlockSpec((tb, D_MODEL), lambda i: (i, 0)),
            _const_spec((D_V, D_MODEL)),
            _const_spec((1, D_MODEL)),
            _const_spec((D_MODEL, D_FF)),
            _const_spec((D_MODEL, D_FF)),
            _const_spec((D_FF, D_MODEL)),
            _const_spec((1, D_MODEL)),
        ],
        out_specs=pl.BlockSpec((tb, D_MODEL), lambda i: (i, 0)),
        out_shape=jax.ShapeDtypeStruct((n, D_MODEL), F32),
        scratch_shapes=[pltpu.VMEM((tb, D_V), BF16), pltpu.VMEM((tb, D_MODEL), BF16),
                        pltpu.VMEM((tb, D_FF), BF16)],
        compiler_params=_params(1),
        name="gdn_out_ffn",
    )(o, z, x, w_out.astype(BF16), norm_w.reshape(1, D_MODEL), w_gate.astype(BF16),
      w_up.astype(BF16), w_down.astype(BF16), norm_final.reshape(1, D_MODEL))


def _trunk(x, conv_buf, s0, p, *, sgu_chunk, gdn_chunk, tb, gdn_tb, core_tb, write_v):
    b, t_len, _ = x.shape
    flat = lambda a: a.reshape(b * t_len, a.shape[-1])
    x0, v_rows = _sgu_layer(flat(x), p["norm_mix"][0], p["sgu_w_in"][0], p["sgu_ln_g"][0],
                            p["sgu_ln_b"][0], p["sgu_w_s"][0], p["sgu_b_s"][0], p["sgu_w_out"][0],
                            chunk=sgu_chunk, tb=tb, write_v=write_v)
    x1 = _ffn_layer(x0, p["norm_ffn"][0], p["ffn_w_gate"][0], p["ffn_w_up"][0], p["ffn_w_down"][0], tb=tb)
    q, k, v, z, beta_t, g_t, conv_tail = _gdn_proj(
        x1.reshape(b, t_len, D_MODEL), conv_buf, p["norm_mix"][1], p["gdn_w_in"][0],
        p["gdn_w_conv"][0], p["gdn_a_log"][0], p["gdn_dt_bias"][0], chunk=gdn_chunk, tb=gdn_tb)
    o, s_new = _gdn_core(q, k, v, beta_t, g_t, s0, p["gdn_w_onorm"][0], chunk=gdn_chunk, tb=core_tb)
    y = _gdn_tail(flat(o), flat(z), x1, p["gdn_w_out"][0], p["norm_ffn"][1], p["ffn_w_gate"][1],
                  p["ffn_w_up"][1], p["ffn_w_down"][1], p["norm_final"], tb=tb)
    conv_new = conv_tail[:, SUBLANES - (CONV_K - 1):, :]
    if write_v:
        v_rows = v_rows.reshape(b, t_len, D_SGU)
    return y.reshape(b, t_len, D_MODEL), s_new, conv_new, v_rows


def kernel(x_prompt, x_sample, state_gdn, state_conv, norm_mix, norm_ffn, norm_final, sgu_w_in, sgu_ln_g, sgu_ln_b, sgu_w_s, sgu_b_s, sgu_w_out, gdn_w_in, gdn_w_conv, gdn_a_log, gdn_dt_bias, gdn_w_onorm, gdn_w_out, ffn_w_gate, ffn_w_up, ffn_w_down):
    p = dict(norm_mix=norm_mix, norm_ffn=norm_ffn, norm_final=norm_final, sgu_w_in=sgu_w_in,
             sgu_ln_g=sgu_ln_g, sgu_ln_b=sgu_ln_b, sgu_w_s=sgu_w_s, sgu_b_s=sgu_b_s,
             sgu_w_out=sgu_w_out, gdn_w_in=gdn_w_in, gdn_w_conv=gdn_w_conv, gdn_a_log=gdn_a_log,
             gdn_dt_bias=gdn_dt_bias, gdn_w_onorm=gdn_w_onorm, gdn_w_out=gdn_w_out,
             ffn_w_gate=ffn_w_gate, ffn_w_up=ffn_w_up, ffn_w_down=ffn_w_down)
    bp = x_prompt.shape[0]
    dec_seq = x_sample.shape[1]
    zero_conv = jnp.zeros((bp, CONV_K - 1, D_QKV), F32)
    zero_state = jnp.zeros((bp, V_HEADS, HEAD_DIM, HEAD_DIM), F32)
    yp, sp, cp, _ = _trunk(x_prompt, zero_conv, zero_state, p, sgu_chunk=SGU_CHUNK,
                           gdn_chunk=2 * GDN_CHUNK, tb=256, gdn_tb=256, core_tb=256, write_v=False)
    ys, ss, cs, vs = _trunk(x_sample, state_conv[0], state_gdn[0], p, sgu_chunk=dec_seq,
                            gdn_chunk=dec_seq, tb=256, gdn_tb=dec_seq, core_tb=dec_seq, write_v=True)
    return (yp, ys, sp[None], cp[None], ss[None], cs[None], vs[None])
```

```python
import functools
import math

import jax
import jax.numpy as jnp
from jax import lax
from jax.experimental import pallas as pl
from jax.experimental.pallas import tpu as pltpu

D_MODEL = 1024
D_SGU = 2048
SGU_GROUPS = 8
SGU_GROUP_DIM = D_SGU // SGU_GROUPS
SGU_CHUNK = 128
HEAD_DIM = 128
K_HEADS = 8
V_HEADS = 16
D_QK = K_HEADS * HEAD_DIM
D_V = V_HEADS * HEAD_DIM
D_QKV = 2 * D_QK + D_V
CONV_K = 4
GDN_CHUNK = 64
D_FF = 2816
EPS = 1e-6
LN_EPS = 1e-5

VMEM_LIMIT_BYTES = 56 * 1024 * 1024
SUBLANES = 8
COL_BLOCK = 256
LANE_PAIR_TILE = 512
INV_BASE = 8

F32 = jnp.float32
BF16 = jnp.bfloat16
HIGHEST = lax.Precision.HIGHEST


def _const_spec(shape):
    nd = len(shape)
    return pl.BlockSpec(shape, lambda *_: (0,) * nd, pipeline_mode=pl.Buffered(1))


def _mxu_weight(w):
    wb = w.astype(BF16)
    if w.shape[-1] % LANE_PAIR_TILE == 0:
        wb = jnp.pad(wb, ((0, 0), (0, COL_BLOCK)))
    return wb


def _params(n_grid):
    return pltpu.CompilerParams(dimension_semantics=("arbitrary",) * n_grid,
                                vmem_limit_bytes=VMEM_LIMIT_BYTES)


def _rms_scale(x):
    return lax.rsqrt(jnp.mean(x * x, axis=-1, keepdims=True) + EPS)


def _silu(x):
    return x * jax.nn.sigmoid(x)


def _gelu_tanh(x):
    return 0.5 * x * (1.0 + jnp.tanh(math.sqrt(2.0 / math.pi) * (x + 0.044715 * (x * x * x))))


def _softplus(x):
    return jnp.maximum(x, 0.0) + jnp.log1p(jnp.exp(-jnp.abs(x)))


def _dot(a, b):
    return jnp.dot(a, b, preferred_element_type=F32)


def _dot_nt(a, b, precision=None):
    return lax.dot_general(a, b, (((1,), (1,)), ((), ())), precision=precision,
                           preferred_element_type=F32)


def _dot_tn(a, b):
    return lax.dot_general(a, b, (((0,), (0,)), ((), ())), preferred_element_type=F32)


def _residual_dot_store(x_ref, a_ref, w_ref, o_ref):
    ssq = jnp.zeros((x_ref.shape[0], 1), F32)
    for c in range(0, x_ref.shape[1], COL_BLOCK):
        cs = slice(c, c + COL_BLOCK)
        y = x_ref[:, cs] + _dot(a_ref[...], w_ref[:, cs])
        o_ref[:, cs] = y
        ssq = ssq + jnp.sum(y * y, axis=-1, keepdims=True)
    return ssq


def _ffn_store(x_ref, nw_ref, wg_ref, wu_ref, wd_ref, hb_ref, act_ref, o_ref, nf_ref):
    x = x_ref[...]
    hb_ref[...] = (x * _rms_scale(x) * nw_ref[...]).astype(BF16)
    for c in range(0, D_FF, COL_BLOCK):
        g = _dot(hb_ref[...], wg_ref[:, c:c + COL_BLOCK])
        u = _dot(hb_ref[...], wu_ref[:, c:c + COL_BLOCK])
        act_ref[:, c:c + COL_BLOCK] = (_silu(g) * u).astype(BF16)
    ssq = _residual_dot_store(x_ref, act_ref, wd_ref, o_ref)
    if nf_ref is not None:
        y = o_ref[...]
        o_ref[...] = y * lax.rsqrt(ssq * (1.0 / D_MODEL) + EPS) * nf_ref[...]


def _sgu_kernel(x_ref, nw_ref, win_ref, lng_ref, lnb_ref, ws_ref, bs_ref, wout_ref,
                *rest, chunk, write_v):
    if write_v:
        o_ref, v_ref, hb_ref, u_ref, vv_ref, gated_ref = rest
    else:
        o_ref, hb_ref, u_ref, vv_ref, gated_ref = rest
        v_ref = None
    tb = x_ref.shape[0]
    x = x_ref[...]
    hb_ref[...] = (x * _rms_scale(x) * nw_ref[...]).astype(BF16)

    vsum = jnp.zeros((tb, 1), F32)
    for c in range(0, 2 * D_SGU, COL_BLOCK):
        uv = _gelu_tanh(_dot(hb_ref[...], win_ref[:, c:c + COL_BLOCK]))
        if c < D_SGU:
            u_ref[:, c:c + COL_BLOCK] = uv
        else:
            vv_ref[:, c - D_SGU:c - D_SGU + COL_BLOCK] = uv
            vsum = vsum + jnp.sum(uv, axis=-1, keepdims=True)
    mu = vsum * (1.0 / D_SGU)
    vvar = jnp.zeros((tb, 1), F32)
    for c in range(0, D_SGU, COL_BLOCK):
        d = vv_ref[:, c:c + COL_BLOCK] - mu
        vvar = vvar + jnp.sum(d * d, axis=-1, keepdims=True)
    rstd = lax.rsqrt(vvar * (1.0 / D_SGU) + LN_EPS)

    row = lax.broadcasted_iota(jnp.int32, (chunk, chunk), 0)
    col = lax.broadcasted_iota(jnp.int32, (chunk, chunk), 1)
    causal = row >= col
    for g in range(SGU_GROUPS):
        c0 = g * SGU_GROUP_DIM
        cs = slice(c0, c0 + SGU_GROUP_DIM)
        vn = (vv_ref[:, cs] - mu) * rstd * lng_ref[:, cs] + lnb_ref[:, cs]
        if write_v:
            v_ref[:, cs] = vn
        vnb = vn.astype(BF16)
        wsg = jnp.where(causal, ws_ref[g], 0.0).astype(BF16)
        bias = bs_ref[:, g:g + 1]
        for r in range(0, tb, chunk):
            mixed = _dot(wsg, vnb[r:r + chunk]) + bias
            gated_ref[r:r + chunk, cs] = (u_ref[r:r + chunk, cs] * mixed).astype(BF16)
    _residual_dot_store(x_ref, gated_ref, wout_ref, o_ref)


def _sgu_layer(x, norm_w, w_in, ln_g, ln_b, w_s, b_s, w_out, *, chunk, tb, write_v):
    n = x.shape[0]
    ws = w_s[:, :chunk, :chunk]
    bs_t = b_s[:, :chunk].T
    out_shape = [jax.ShapeDtypeStruct((n, D_MODEL), F32)]
    out_specs = [pl.BlockSpec((tb, D_MODEL), lambda i: (i, 0))]
    if write_v:
        out_shape.append(jax.ShapeDtypeStruct((n, D_SGU), F32))
        out_specs.append(pl.BlockSpec((tb, D_SGU), lambda i: (i, 0)))
    consts = (norm_w.reshape(1, D_MODEL), _mxu_weight(w_in), ln_g.reshape(1, D_SGU),
              ln_b.reshape(1, D_SGU), ws, bs_t, _mxu_weight(w_out))
    res = pl.pallas_call(
        functools.partial(_sgu_kernel, chunk=chunk, write_v=write_v),
        grid=(n // tb,),
        in_specs=[pl.BlockSpec((tb, D_MODEL), lambda i: (i, 0))] + [_const_spec(a.shape) for a in consts],
        out_specs=out_specs,
        out_shape=out_shape,
        scratch_shapes=[
            pltpu.VMEM((tb, D_MODEL), BF16),
            pltpu.VMEM((tb, D_SGU), F32),
            pltpu.VMEM((tb, D_SGU), F32),
            pltpu.VMEM((tb, D_SGU), BF16),
        ],
        compiler_params=_params(1),
        name="sgu_mixer",
    )(x, *consts)
    return res if write_v else (res[0], None)


def _ffn_kernel(x_ref, nw_ref, wg_ref, wu_ref, wd_ref, o_ref, hb_ref, act_ref):
    _ffn_store(x_ref, nw_ref, wg_ref, wu_ref, wd_ref, hb_ref, act_ref, o_ref, None)


def _ffn_layer(x, norm_w, w_gate, w_up, w_down, *, tb):
    n = x.shape[0]
    consts = (norm_w.reshape(1, D_MODEL), _mxu_weight(w_gate), _mxu_weight(w_up), _mxu_weight(w_down))
    return pl.pallas_call(
        _ffn_kernel,
        grid=(n // tb,),
        in_specs=[pl.BlockSpec((tb, D_MODEL), lambda i: (i, 0))] + [_const_spec(a.shape) for a in consts],
        out_specs=pl.BlockSpec((tb, D_MODEL), lambda i: (i, 0)),
        out_shape=jax.ShapeDtypeStruct((n, D_MODEL), F32),
        scratch_shapes=[pltpu.VMEM((tb, D_MODEL), BF16), pltpu.VMEM((tb, D_FF), BF16)],
        compiler_params=_params(1),
        name="ffn",
    )(x, *consts)


def _gdn_proj_kernel(x_ref, cbuf_ref, nw_ref, wqkv_ref, wz_ref, wbat_ref, wconv_ref,
                     alog_ref, dtb_ref,
                     q_ref, k_ref, v_ref, z_ref, beta_ref, g_ref, cnew_ref,
                     hb_ref, hist_ref, *, chunk):
    tb = x_ref.shape[1]
    t = pl.program_id(1)
    x = x_ref[0]
    hb_ref[...] = (x * _rms_scale(x) * nw_ref[...]).astype(BF16)

    @pl.when(t == 0)
    def _():
        hist_ref[...] = cbuf_ref[0]

    for c in range(0, D_QKV, COL_BLOCK):
        cs = slice(c, c + COL_BLOCK)
        pre = _dot(hb_ref[...], wqkv_ref[:, cs])
        padded = jnp.concatenate([hist_ref[:, cs], pre], axis=0)
        hist_ref[:, cs] = pre[tb - SUBLANES:, :]
        cnew_ref[0, :, cs] = pre[tb - SUBLANES:, :]
        acc = padded * wconv_ref[0:1, cs]
        for i in range(1, CONV_K):
            acc = pltpu.roll(acc, 1, axis=0) + padded * wconv_ref[i:i + 1, cs]
        y = _silu(acc[SUBLANES:, :])
        if c < 2 * D_QK:
            dst, c_dst, scale = (q_ref, c, HEAD_DIM ** -0.5) if c < D_QK else (k_ref, c - D_QK, 1.0)
            for h in range(0, COL_BLOCK, HEAD_DIM):
                yh = y[:, h:h + HEAD_DIM]
                inv = lax.rsqrt(jnp.sum(yh * yh, axis=-1, keepdims=True) + EPS) * scale
                dst[0, :, c_dst + h:c_dst + h + HEAD_DIM] = (yh * inv).astype(BF16)
        else:
            v_ref[0, :, c - 2 * D_QK:c - 2 * D_QK + COL_BLOCK] = y
        blk = c // COL_BLOCK
        if blk % 2 == 1:
            cz = (blk // 2) * COL_BLOCK
            z_ref[0, :, cz:cz + COL_BLOCK] = _dot(hb_ref[...], wz_ref[:, cz:cz + COL_BLOCK])

    ba_t = _dot_nt(wbat_ref[...], hb_ref[...])
    beta_t = jax.nn.sigmoid(ba_t[:V_HEADS])
    loga_t = -jnp.exp(alog_ref[...]) * _softplus(ba_t[V_HEADS:] + dtb_ref[...])
    row = lax.broadcasted_iota(jnp.int32, (chunk, chunk), 0)
    col = lax.broadcasted_iota(jnp.int32, (chunk, chunk), 1)
    upper = (row <= col).astype(F32)
    for j in range(tb // chunk):
        js = slice(j * chunk, (j + 1) * chunk)
        beta_ref[0, j] = beta_t[:, js]
        g_ref[0, j] = jnp.dot(loga_t[:, js], upper, precision=HIGHEST, preferred_element_type=F32)


def _gdn_proj(x, conv_buf, norm_w, w_in, w_conv, a_log, dt_bias, *, chunk, tb):
    b, t_len, _ = x.shape
    nt = t_len // tb
    npc = tb // chunk
    cbuf = jnp.pad(conv_buf, ((0, 0), (SUBLANES - (CONV_K - 1), 0), (0, 0)))
    w_ba_t = w_in[:, D_QKV + D_V:].T.astype(BF16)
    consts = (norm_w.reshape(1, D_MODEL), _mxu_weight(w_in[:, :D_QKV]),
              _mxu_weight(w_in[:, D_QKV:D_QKV + D_V]), w_ba_t, w_conv,
              a_log.reshape(V_HEADS, 1), dt_bias.reshape(V_HEADS, 1))
    tok = lambda width: pl.BlockSpec((1, tb, width), lambda i, j: (i, j, 0))
    rows = pl.BlockSpec((1, npc, V_HEADS, chunk), lambda i, j: (i, j, 0, 0))
    return pl.pallas_call(
        functools.partial(_gdn_proj_kernel, chunk=chunk),
        grid=(b, nt),
        in_specs=[tok(D_MODEL), pl.BlockSpec((1, SUBLANES, D_QKV), lambda i, j: (i, 0, 0))]
        + [_const_spec(a.shape) for a in consts],
        out_specs=[tok(D_QK), tok(D_QK), tok(D_V), tok(D_V), rows, rows,
                   pl.BlockSpec((1, SUBLANES, D_QKV), lambda i, j: (i, 0, 0))],
        out_shape=[
            jax.ShapeDtypeStruct((b, t_len, D_QK), BF16),
            jax.ShapeDtypeStruct((b, t_len, D_QK), BF16),
            jax.ShapeDtypeStruct((b, t_len, D_V), F32),
            jax.ShapeDtypeStruct((b, t_len, D_V), F32),
            jax.ShapeDtypeStruct((b, t_len // chunk, V_HEADS, chunk), F32),
            jax.ShapeDtypeStruct((b, t_len // chunk, V_HEADS, chunk), F32),
            jax.ShapeDtypeStruct((b, SUBLANES, D_QKV), F32),
        ],
        scratch_shapes=[pltpu.VMEM((tb, D_MODEL), BF16),
                        pltpu.VMEM((SUBLANES, D_QKV), F32)],
        compiler_params=_params(2),
        name="gdn_proj",
    )(x, cbuf, *consts)


def _gdn_core_kernel(q_ref, k_ref, v_ref, beta_ref, g_ref, s0_ref, won_ref, o_ref, s_ref,
                     t_ref, p_ref, x_ref, attn_ref, kdt_ref, col_ref, ks_ref, dl_ref, *, chunk):
    @pl.when(pl.program_id(1) == 0)
    def _():
        s_ref[...] = s0_ref[...]

    n_chunks = q_ref.shape[1] // chunk
    row = lax.broadcasted_iota(jnp.int32, (chunk, chunk), 0)
    col = lax.broadcasted_iota(jnp.int32, (chunk, chunk), 1)
    incl = row >= col
    strict = row > col
    eye = (row == col).astype(F32)
    diag_blk = (row // INV_BASE) == (col // INV_BASE)
    base_factors = INV_BASE.bit_length() - 1
    problems = [(c, h) for c in range(n_chunks) for h in range(V_HEADS)]
    slot = lambda c, h: c * V_HEADS + h

    def kq_of(c, pr):
        rs = slice(c * chunk, (c + 1) * chunk)
        hs = slice(pr * HEAD_DIM, (pr + 1) * HEAD_DIM)
        return jnp.concatenate([k_ref[0, rs, hs], q_ref[0, rs, hs]], axis=0)

    for c in range(n_chunks):
        g_rows = g_ref[0, c]
        rows_t = jnp.concatenate([beta_ref[0, c], g_rows, jnp.exp(g_rows)], axis=0)
        col_ref[c] = rows_t.T

    for c in range(n_chunks):
        for pr in range(K_HEADS):
            kq = kq_of(c, pr)
            kq_kt = _dot_nt(kq, kq[:chunk])
            k_t = kq[:chunk].astype(F32).T
            for h in (2 * pr, 2 * pr + 1):
                g_row = g_ref[0, c, h:h + 1, :]
                k_dec = jnp.exp(g_row[:, chunk - 1:chunk] - g_row)
                kdt_ref[slot(c, h)] = (k_t * k_dec).astype(BF16)
                b_col = col_ref[c, :, h:h + 1]
                g_col = col_ref[c, :, V_HEADS + h:V_HEADS + h + 1]
                decay = jnp.where(incl, jnp.exp(jnp.where(incl, g_col - g_row, 0.0)), 0.0)
                x_mat = jnp.where(strict, -(b_col * kq_kt[:chunk] * decay), 0.0)
                x_diag = jnp.where(diag_blk, x_mat, 0.0)
                x_ref[slot(c, h)] = x_mat.astype(BF16)
                t_ref[slot(c, h)] = eye + x_diag
                p_ref[slot(c, h)] = x_diag.astype(BF16)
                attn_ref[slot(c, h)] = jnp.where(incl, kq_kt[chunk:] * decay, 0.0).astype(BF16)

    for c, h in problems:
        xd = p_ref[slot(c, h)]
        p_ref[slot(c, h)] = _dot(xd, xd).astype(BF16)
    for j in range(1, base_factors):
        for c, h in problems:
            pb = p_ref[slot(c, h)]
            tm = t_ref[slot(c, h)]
            if j < base_factors - 1:
                pt = _dot(pb, jnp.concatenate([pb, tm.astype(BF16)], axis=1))
                p_ref[slot(c, h)] = pt[:, :chunk].astype(BF16)
                t_ref[slot(c, h)] = tm + pt[:, chunk:]
            else:
                t_ref[slot(c, h)] = tm + _dot(pb, tm.astype(BF16))
    width = INV_BASE
    while width < chunk:
        off = ((row // (2 * width)) == (col // (2 * width))) & ((row // width) != (col // width))
        for c, h in problems:
            tb16 = t_ref[slot(c, h)].astype(BF16)
            w_mat = _dot(jnp.where(off, x_ref[slot(c, h)], jnp.zeros((), BF16)), tb16)
            t_new = t_ref[slot(c, h)] + _dot(tb16, w_mat.astype(BF16))
            if 2 * width < chunk:
                t_ref[slot(c, h)] = t_new
            else:
                p_ref[slot(c, h)] = t_new.astype(BF16)
        width *= 2

    for c in range(n_chunks):
        rs = slice(c * chunk, (c + 1) * chunk)
        for pr in range(K_HEADS):
            s_pair = jnp.concatenate([s_ref[0, 2 * pr], s_ref[0, 2 * pr + 1]], axis=1).astype(BF16)
            ks_ref[pr] = _dot(kq_of(c, pr), s_pair)
        for h in range(V_HEADS):
            es = slice((h % 2) * HEAD_DIM, (h % 2 + 1) * HEAD_DIM)
            b_col = col_ref[c, :, h:h + 1]
            gam = col_ref[c, :, 2 * V_HEADS + h:2 * V_HEADS + h + 1]
            rhs = b_col * (v_ref[0, rs, h * HEAD_DIM:(h + 1) * HEAD_DIM] - gam * ks_ref[h // 2, :chunk, es])
            dl_ref[h] = _dot(p_ref[slot(c, h)], rhs.astype(BF16)).astype(BF16)
        for h in range(V_HEADS):
            s_decay = jnp.exp(g_ref[0, c, h:h + 1, chunk - 1:chunk])
            s_ref[0, h] = s_decay * s_ref[0, h] + _dot(kdt_ref[slot(c, h)], dl_ref[h])
        for h in range(V_HEADS):
            es = slice((h % 2) * HEAD_DIM, (h % 2 + 1) * HEAD_DIM)
            gam = col_ref[c, :, 2 * V_HEADS + h:2 * V_HEADS + h + 1]
            o = gam * ks_ref[h // 2, chunk:, es] + _dot(attn_ref[slot(c, h)], dl_ref[h])
            o_ref[0, rs, h * HEAD_DIM:(h + 1) * HEAD_DIM] = o * _rms_scale(o) * won_ref[...]


def _gdn_core(q, k, v, beta_t, g_t, s0, w_onorm, *, chunk, tb):
    b, t_len, _ = q.shape
    nt = t_len // tb
    npc = tb // chunk
    tok = lambda width: pl.BlockSpec((1, tb, width), lambda i, j: (i, j, 0))
    rows = pl.BlockSpec((1, npc, V_HEADS, chunk), lambda i, j: (i, j, 0, 0))
    state = pl.BlockSpec((1, V_HEADS, HEAD_DIM, HEAD_DIM), lambda i, j: (i, 0, 0, 0))
    return pl.pallas_call(
        functools.partial(_gdn_core_kernel, chunk=chunk),
        grid=(b, nt),
        in_specs=[tok(D_QK), tok(D_QK), tok(D_V), rows, rows, state, _const_spec((1, HEAD_DIM))],
        out_specs=[tok(D_V), state],
        out_shape=[jax.ShapeDtypeStruct((b, t_len, D_V), F32),
                   jax.ShapeDtypeStruct((b, V_HEADS, HEAD_DIM, HEAD_DIM), F32)],
        scratch_shapes=[pltpu.VMEM((npc * V_HEADS, chunk, chunk), F32),
                        pltpu.VMEM((npc * V_HEADS, chunk, chunk), BF16),
                        pltpu.VMEM((npc * V_HEADS, chunk, chunk), BF16),
                        pltpu.VMEM((npc * V_HEADS, chunk, chunk), BF16),
                        pltpu.VMEM((npc * V_HEADS, HEAD_DIM, chunk), BF16),
                        pltpu.VMEM((npc, chunk, 3 * V_HEADS), F32),
                        pltpu.VMEM((K_HEADS, 2 * chunk, 2 * HEAD_DIM), F32),
                        pltpu.VMEM((V_HEADS, chunk, HEAD_DIM), BF16)],
        compiler_params=_params(2),
        name="gdn_core",
    )(q, k, v, beta_t, g_t, s0, w_onorm.reshape(1, HEAD_DIM))


def _gdn_tail_kernel(o_ref, z_ref, x_ref, wout_ref, nw_ref, wg_ref, wu_ref, wd_ref, nf_ref,
                     y_ref, gated_ref, mid_ref, hb_ref, act_ref):
    for c in range(0, D_V, COL_BLOCK):
        cs = slice(c, c + COL_BLOCK)
        gated_ref[:, cs] = (o_ref[:, cs] * _silu(z_ref[:, cs])).astype(BF16)
    _residual_dot_store(x_ref, gated_ref, wout_ref, mid_ref)
    _ffn_store(mid_ref, nw_ref, wg_ref, wu_ref, wd_ref, hb_ref, act_ref, y_ref, nf_ref)


def _gdn_tail(o, z, x, w_out, norm_w, w_gate, w_up, w_down, norm_final, *, tb):
    n = x.shape[0]
    consts = (_mxu_weight(w_out), norm_w.reshape(1, D_MODEL), _mxu_weight(w_gate), _mxu_weight(w_up),
              _mxu_weight(w_down), norm_final.reshape(1, D_MODEL))
    return pl.pallas_call(
        _gdn_tail_kernel,
        grid=(n // tb,),
        in_specs=[pl.BlockSpec((tb, D_V), lambda i: (i, 0)), pl.BlockSpec((tb, D_V), lambda i: (i, 0)),
                  pl.BlockSpec((tb, D_MODEL), lambda i: (i, 0))] + [_const_spec(a.shape) for a in consts],
        out_specs=pl.BlockSpec((tb, D_MODEL), lambda i: (i, 0)),
        out_shape=jax.ShapeDtypeStruct((n, D_MODEL), F32),
        scratch_shapes=[pltpu.VMEM((tb, D_V), BF16), pltpu.VMEM((tb, D_MODEL), F32),
                        pltpu.VMEM((tb, D_MODEL), BF16), pltpu.VMEM((tb, D_FF), BF16)],
        compiler_params=_params(1),
        name="gdn_out_ffn",
    )(o, z, x, *consts)


def _trunk(x, conv_buf, s0, p, *, sgu_chunk, gdn_chunk, tb, gdn_tb, core_tb, write_v):
    b, t_len, _ = x.shape
    flat = lambda a: a.reshape(b * t_len, a.shape[-1])
    x0, v_rows = _sgu_layer(flat(x), p["norm_mix"][0], p["sgu_w_in"][0], p["sgu_ln_g"][0],
                            p["sgu_ln_b"][0], p["sgu_w_s"][0], p["sgu_b_s"][0], p["sgu_w_out"][0],
                            chunk=sgu_chunk, tb=tb, write_v=write_v)
    x1 = _ffn_layer(x0, p["norm_ffn"][0], p["ffn_w_gate"][0], p["ffn_w_up"][0], p["ffn_w_down"][0], tb=tb)
    q, k, v, z, beta_t, g_t, conv_tail = _gdn_proj(
        x1.reshape(b, t_len, D_MODEL), conv_buf, p["norm_mix"][1], p["gdn_w_in"][0],
        p["gdn_w_conv"][0], p["gdn_a_log"][0], p["gdn_dt_bias"][0], chunk=gdn_chunk, tb=gdn_tb)
    o, s_new = _gdn_core(q, k, v, beta_t, g_t, s0, p["gdn_w_onorm"][0], chunk=gdn_chunk, tb=core_tb)
    y = _gdn_tail(flat(o), flat(z), x1, p["gdn_w_out"][0], p["norm_ffn"][1], p["ffn_w_gate"][1],
                  p["ffn_w_up"][1], p["ffn_w_down"][1], p["norm_final"], tb=tb)
    conv_new = conv_tail[:, SUBLANES - (CONV_K - 1):, :]
    if write_v:
        v_rows = v_rows.reshape(b, t_len, D_SGU)
    return y.reshape(b, t_len, D_MODEL), s_new, conv_new, v_rows


def kernel(x_prompt, x_sample, state_gdn, state_conv, norm_mix, norm_ffn, norm_final, sgu_w_in, sgu_ln_g, sgu_ln_b, sgu_w_s, sgu_b_s, sgu_w_out, gdn_w_in, gdn_w_conv, gdn_a_log, gdn_dt_bias, gdn_w_onorm, gdn_w_out, ffn_w_gate, ffn_w_up, ffn_w_down):
    p = dict(norm_mix=norm_mix, norm_ffn=norm_ffn, norm_final=norm_final, sgu_w_in=sgu_w_in,
             sgu_ln_g=sgu_ln_g, sgu_ln_b=sgu_ln_b, sgu_w_s=sgu_w_s, sgu_b_s=sgu_b_s,
             sgu_w_out=sgu_w_out, gdn_w_in=gdn_w_in, gdn_w_conv=gdn_w_conv, gdn_a_log=gdn_a_log,
             gdn_dt_bias=gdn_dt_bias, gdn_w_onorm=gdn_w_onorm, gdn_w_out=gdn_w_out,
             ffn_w_gate=ffn_w_gate, ffn_w_up=ffn_w_up, ffn_w_down=ffn_w_down)
    bp = x_prompt.shape[0]
    dec_seq = x_sample.shape[1]
    zero_conv = jnp.zeros((bp, CONV_K - 1, D_QKV), F32)
    zero_state = jnp.zeros((bp, V_HEADS, HEAD_DIM, HEAD_DIM), F32)
    yp, sp, cp, _ = _trunk(x_prompt, zero_conv, zero_state, p, sgu_chunk=SGU_CHUNK,
                           gdn_chunk=2 * GDN_CHUNK, tb=256, gdn_tb=256, core_tb=256, write_v=False)
    ys, ss, cs, vs = _trunk(x_sample, state_conv[0], state_gdn[0], p, sgu_chunk=dec_seq,
                            gdn_chunk=dec_seq, tb=256, gdn_tb=dec_seq, core_tb=dec_seq, write_v=True)
    return (yp, ys, sp[None], cp[None], ss[None], cs[None], vs[None])
```

```python
import functools
import math

import jax
import jax.numpy as jnp
from jax import lax
from jax.experimental import pallas as pl
from jax.experimental.pallas import tpu as pltpu

D_MODEL = 1024
D_SGU = 2048
SGU_GROUPS = 8
SGU_GROUP_DIM = D_SGU // SGU_GROUPS
SGU_CHUNK = 128
HEAD_DIM = 128
K_HEADS = 8
V_HEADS = 16
D_QK = K_HEADS * HEAD_DIM
D_V = V_HEADS * HEAD_DIM
D_QKV = 2 * D_QK + D_V
CONV_K = 4
GDN_CHUNK = 64
D_FF = 2816
EPS = 1e-6
LN_EPS = 1e-5

VMEM_LIMIT_BYTES = 56 * 1024 * 1024
SUBLANES = 8
COL_BLOCK = 256
LANE_PAIR_TILE = 512
INV_BASE = 8

F32 = jnp.float32
BF16 = jnp.bfloat16
HIGHEST = lax.Precision.HIGHEST


def _const_spec(shape):
    nd = len(shape)
    return pl.BlockSpec(shape, lambda *_: (0,) * nd, pipeline_mode=pl.Buffered(1))


def _mxu_weight(w):
    wb = w.astype(BF16)
    if w.shape[-1] % LANE_PAIR_TILE == 0:
        wb = jnp.pad(wb, ((0, 0), (0, COL_BLOCK)))
    return wb


def _params(n_grid):
    return pltpu.CompilerParams(dimension_semantics=("arbitrary",) * n_grid,
                                vmem_limit_bytes=VMEM_LIMIT_BYTES)


def _rms_scale(x):
    return lax.rsqrt(jnp.mean(x * x, axis=-1, keepdims=True) + EPS)


def _silu(x):
    return x * jax.nn.sigmoid(x)


def _gelu_tanh(x):
    return 0.5 * x * (1.0 + jnp.tanh(math.sqrt(2.0 / math.pi) * (x + 0.044715 * (x * x * x))))


def _softplus(x):
    return jnp.maximum(x, 0.0) + jnp.log1p(jnp.exp(-jnp.abs(x)))


def _dot(a, b):
    return jnp.dot(a, b, preferred_element_type=F32)


def _dot_nt(a, b, precision=None):
    return lax.dot_general(a, b, (((1,), (1,)), ((), ())), precision=precision,
                           preferred_element_type=F32)


def _dot_tn(a, b):
    return lax.dot_general(a, b, (((0,), (0,)), ((), ())), preferred_element_type=F32)


def _residual_dot_store(x_ref, a_ref, w_ref, o_ref):
    ssq = jnp.zeros((x_ref.shape[0], 1), F32)
    for c in range(0, x_ref.shape[1], COL_BLOCK):
        cs = slice(c, c + COL_BLOCK)
        y = x_ref[:, cs] + _dot(a_ref[...], w_ref[:, cs])
        o_ref[:, cs] = y
        ssq = ssq + jnp.sum(y * y, axis=-1, keepdims=True)
    return ssq


def _ffn_store(x_ref, nw_ref, wg_ref, wu_ref, wd_ref, hb_ref, act_ref, o_ref, nf_ref):
    x = x_ref[...]
    hb_ref[...] = (x * _rms_scale(x) * nw_ref[...]).astype(BF16)
    for c in range(0, D_FF, COL_BLOCK):
        g = _dot(hb_ref[...], wg_ref[:, c:c + COL_BLOCK])
        u = _dot(hb_ref[...], wu_ref[:, c:c + COL_BLOCK])
        act_ref[:, c:c + COL_BLOCK] = (_silu(g) * u).astype(BF16)
    ssq = _residual_dot_store(x_ref, act_ref, wd_ref, o_ref)
    if nf_ref is not None:
        y = o_ref[...]
        o_ref[...] = y * lax.rsqrt(ssq * (1.0 / D_MODEL) + EPS) * nf_ref[...]


def _sgu_kernel(x_ref, nw_ref, win_ref, lng_ref, lnb_ref, ws_ref, bs_ref, wout_ref,
                *rest, chunk, write_v):
    if write_v:
        o_ref, v_ref, hb_ref, u_ref, vv_ref, gated_ref = rest
    else:
        o_ref, hb_ref, u_ref, vv_ref, gated_ref = rest
        v_ref = None
    tb = x_ref.shape[0]
    x = x_ref[...]
    hb_ref[...] = (x * _rms_scale(x) * nw_ref[...]).astype(BF16)

    vsum = jnp.zeros((tb, 1), F32)
    for c in range(0, 2 * D_SGU, COL_BLOCK):
        uv = _gelu_tanh(_dot(hb_ref[...], win_ref[:, c:c + COL_BLOCK]))
        if c < D_SGU:
            u_ref[:, c:c + COL_BLOCK] = uv
        else:
            vv_ref[:, c - D_SGU:c - D_SGU + COL_BLOCK] = uv
            vsum = vsum + jnp.sum(uv, axis=-1, keepdims=True)
    mu = vsum * (1.0 / D_SGU)
    vvar = jnp.zeros((tb, 1), F32)
    for c in range(0, D_SGU, COL_BLOCK):
        d = vv_ref[:, c:c + COL_BLOCK] - mu
        vvar = vvar + jnp.sum(d * d, axis=-1, keepdims=True)
    rstd = lax.rsqrt(vvar * (1.0 / D_SGU) + LN_EPS)

    row = lax.broadcasted_iota(jnp.int32, (chunk, chunk), 0)
    col = lax.broadcasted_iota(jnp.int32, (chunk, chunk), 1)
    causal = row >= col
    for g in range(SGU_GROUPS):
        c0 = g * SGU_GROUP_DIM
        cs = slice(c0, c0 + SGU_GROUP_DIM)
        vn = (vv_ref[:, cs] - mu) * rstd * lng_ref[:, cs] + lnb_ref[:, cs]
        if write_v:
            v_ref[:, cs] = vn
        vnb = vn.astype(BF16)
        wsg = jnp.where(causal, ws_ref[g], 0.0).astype(BF16)
        bias = bs_ref[:, g:g + 1]
        for r in range(0, tb, chunk):
            mixed = _dot(wsg, vnb[r:r + chunk]) + bias
            gated_ref[r:r + chunk, cs] = (u_ref[r:r + chunk, cs] * mixed).astype(BF16)
    _residual_dot_store(x_ref, gated_ref, wout_ref, o_ref)


def _sgu_layer(x, norm_w, w_in, ln_g, ln_b, w_s, b_s, w_out, *, chunk, tb, write_v):
    n = x.shape[0]
    ws = w_s[:, :chunk, :chunk]
    bs_t = b_s[:, :chunk].T
    out_shape = [jax.ShapeDtypeStruct((n, D_MODEL), F32)]
    out_specs = [pl.BlockSpec((tb, D_MODEL), lambda i: (i, 0))]
    if write_v:
        out_shape.append(jax.ShapeDtypeStruct((n, D_SGU), F32))
        out_specs.append(pl.BlockSpec((tb, D_SGU), lambda i: (i, 0)))
    consts = (norm_w.reshape(1, D_MODEL), _mxu_weight(w_in), ln_g.reshape(1, D_SGU),
              ln_b.reshape(1, D_SGU), ws, bs_t, _mxu_weight(w_out))
    res = pl.pallas_call(
        functools.partial(_sgu_kernel, chunk=chunk, write_v=write_v),
        grid=(n // tb,),
        in_specs=[pl.BlockSpec((tb, D_MODEL), lambda i: (i, 0))] + [_const_spec(a.shape) for a in consts],
        out_specs=out_specs,
        out_shape=out_shape,
        scratch_shapes=[
            pltpu.VMEM((tb, D_MODEL), BF16),
            pltpu.VMEM((tb, D_SGU), F32),
            pltpu.VMEM((tb, D_SGU), F32),
            pltpu.VMEM((tb, D_SGU), BF16),
        ],
        compiler_params=_params(1),
        name="sgu_mixer",
    )(x, *consts)
    return res if write_v else (res[0], None)


def _ffn_kernel(x_ref, nw_ref, wg_ref, wu_ref, wd_ref, o_ref, hb_ref, act_ref):
    _ffn_store(x_ref, nw_ref, wg_ref, wu_ref, wd_ref, hb_ref, act_ref, o_ref, None)


def _ffn_layer(x, norm_w, w_gate, w_up, w_down, *, tb):
    n = x.shape[0]
    consts = (norm_w.reshape(1, D_MODEL), _mxu_weight(w_gate), _mxu_weight(w_up), _mxu_weight(w_down))
    return pl.pallas_call(
        _ffn_kernel,
        grid=(n // tb,),
        in_specs=[pl.BlockSpec((tb, D_MODEL), lambda i: (i, 0))] + [_const_spec(a.shape) for a in consts],
        out_specs=pl.BlockSpec((tb, D_MODEL), lambda i: (i, 0)),
        out_shape=jax.ShapeDtypeStruct((n, D_MODEL), F32),
        scratch_shapes=[pltpu.VMEM((tb, D_MODEL), BF16), pltpu.VMEM((tb, D_FF), BF16)],
        compiler_params=_params(1),
        name="ffn",
    )(x, *consts)


def _gdn_proj_kernel(x_ref, cbuf_ref, nw_ref, wqkv_ref, wz_ref, wbat_ref, wconv_ref,
                     alog_ref, dtb_ref,
                     q_ref, k_ref, v_ref, z_ref, beta_ref, g_ref, cnew_ref,
                     hb_ref, hist_ref, *, chunk):
    tb = x_ref.shape[1]
    t = pl.program_id(1)
    x = x_ref[0]
    hb_ref[...] = (x * _rms_scale(x) * nw_ref[...]).astype(BF16)

    @pl.when(t == 0)
    def _():
        hist_ref[...] = cbuf_ref[0]

    for c in range(0, D_QKV, COL_BLOCK):
        cs = slice(c, c + COL_BLOCK)
        pre = _dot(hb_ref[...], wqkv_ref[:, cs])
        padded = jnp.concatenate([hist_ref[:, cs], pre], axis=0)
        hist_ref[:, cs] = pre[tb - SUBLANES:, :]
        cnew_ref[0, :, cs] = pre[tb - SUBLANES:, :]
        acc = padded * wconv_ref[0:1, cs]
        for i in range(1, CONV_K):
            acc = pltpu.roll(acc, 1, axis=0) + padded * wconv_ref[i:i + 1, cs]
        y = _silu(acc[SUBLANES:, :])
        if c < 2 * D_QK:
            dst, c_dst, scale = (q_ref, c, HEAD_DIM ** -0.5) if c < D_QK else (k_ref, c - D_QK, 1.0)
            for h in range(0, COL_BLOCK, HEAD_DIM):
                yh = y[:, h:h + HEAD_DIM]
                inv = lax.rsqrt(jnp.sum(yh * yh, axis=-1, keepdims=True) + EPS) * scale
                dst[0, :, c_dst + h:c_dst + h + HEAD_DIM] = (yh * inv).astype(BF16)
        else:
            v_ref[0, :, c - 2 * D_QK:c - 2 * D_QK + COL_BLOCK] = y.astype(BF16)
        blk = c // COL_BLOCK
        if blk % 2 == 1:
            cz = (blk // 2) * COL_BLOCK
            z_ref[0, :, cz:cz + COL_BLOCK] = _dot(hb_ref[...], wz_ref[:, cz:cz + COL_BLOCK]).astype(BF16)

    ba_t = _dot_nt(wbat_ref[...], hb_ref[...])
    beta_t = jax.nn.sigmoid(ba_t[:V_HEADS])
    loga_t = -jnp.exp(alog_ref[...]) * _softplus(ba_t[V_HEADS:] + dtb_ref[...])
    row = lax.broadcasted_iota(jnp.int32, (chunk, chunk), 0)
    col = lax.broadcasted_iota(jnp.int32, (chunk, chunk), 1)
    upper = (row <= col).astype(F32)
    for j in range(tb // chunk):
        js = slice(j * chunk, (j + 1) * chunk)
        beta_ref[0, j] = beta_t[:, js]
        g_ref[0, j] = jnp.dot(loga_t[:, js], upper, precision=HIGHEST, preferred_element_type=F32)


def _gdn_proj(x, conv_buf, norm_w, w_in, w_conv, a_log, dt_bias, *, chunk, tb):
    b, t_len, _ = x.shape
    nt = t_len // tb
    npc = tb // chunk
    cbuf = jnp.pad(conv_buf, ((0, 0), (SUBLANES - (CONV_K - 1), 0), (0, 0)))
    w_ba_t = w_in[:, D_QKV + D_V:].T.astype(BF16)
    consts = (norm_w.reshape(1, D_MODEL), _mxu_weight(w_in[:, :D_QKV]),
              _mxu_weight(w_in[:, D_QKV:D_QKV + D_V]), w_ba_t, w_conv,
              a_log.reshape(V_HEADS, 1), dt_bias.reshape(V_HEADS, 1))
    tok = lambda width: pl.BlockSpec((1, tb, width), lambda i, j: (i, j, 0))
    rows = pl.BlockSpec((1, npc, V_HEADS, chunk), lambda i, j: (i, j, 0, 0))
    return pl.pallas_call(
        functools.partial(_gdn_proj_kernel, chunk=chunk),
        grid=(b, nt),
        in_specs=[tok(D_MODEL), pl.BlockSpec((1, SUBLANES, D_QKV), lambda i, j: (i, 0, 0))]
        + [_const_spec(a.shape) for a in consts],
        out_specs=[tok(D_QK), tok(D_QK), tok(D_V), tok(D_V), rows, rows,
                   pl.BlockSpec((1, SUBLANES, D_QKV), lambda i, j: (i, 0, 0))],
        out_shape=[
            jax.ShapeDtypeStruct((b, t_len, D_QK), BF16),
            jax.ShapeDtypeStruct((b, t_len, D_QK), BF16),
            jax.ShapeDtypeStruct((b, t_len, D_V), BF16),
            jax.ShapeDtypeStruct((b, t_len, D_V), BF16),
            jax.ShapeDtypeStruct((b, t_len // chunk, V_HEADS, chunk), F32),
            jax.ShapeDtypeStruct((b, t_len // chunk, V_HEADS, chunk), F32),
            jax.ShapeDtypeStruct((b, SUBLANES, D_QKV), F32),
        ],
        scratch_shapes=[pltpu.VMEM((tb, D_MODEL), BF16),
                        pltpu.VMEM((SUBLANES, D_QKV), F32)],
        compiler_params=_params(2),
        name="gdn_proj",
    )(x, cbuf, *consts)


def _gdn_core_kernel(q_ref, k_ref, v_ref, beta_ref, g_ref, s0_ref, won_ref, o_ref, s_ref,
                     t_ref, p_ref, x_ref, attn_ref, kdt_ref, col_ref, ks_ref, dl_ref, *, chunk):
    @pl.when(pl.program_id(1) == 0)
    def _():
        s_ref[...] = s0_ref[...]

    n_chunks = q_ref.shape[1] // chunk
    row = lax.broadcasted_iota(jnp.int32, (chunk, chunk), 0)
    col = lax.broadcasted_iota(jnp.int32, (chunk, chunk), 1)
    incl = row >= col
    strict = row > col
    eye = (row == col).astype(F32)
    diag_blk = (row // INV_BASE) == (col // INV_BASE)
    base_factors = INV_BASE.bit_length() - 1
    slot = lambda c, h: c * V_HEADS + h

    def kq_of(c, pr):
        rs = slice(c * chunk, (c + 1) * chunk)
        hs = slice(pr * HEAD_DIM, (pr + 1) * HEAD_DIM)
        return jnp.concatenate([k_ref[0, rs, hs], q_ref[0, rs, hs]], axis=0)

    for c in range(n_chunks):
        g_rows = g_ref[0, c]
        rows_t = jnp.concatenate([beta_ref[0, c], g_rows, jnp.exp(g_rows)], axis=0)
        col_ref[c] = rows_t.T

    def build_operands(c, pr):
        kq = kq_of(c, pr)
        kq_kt = _dot_nt(kq, kq[:chunk])
        k_t = kq[:chunk].astype(F32).T
        for h in (2 * pr, 2 * pr + 1):
            g_row = g_ref[0, c, h:h + 1, :]
            k_dec = jnp.exp(g_row[:, chunk - 1:chunk] - g_row)
            kdt_ref[slot(c, h)] = (k_t * k_dec).astype(BF16)
            b_col = col_ref[c, :, h:h + 1]
            g_col = col_ref[c, :, V_HEADS + h:V_HEADS + h + 1]
            decay = jnp.where(incl, jnp.exp(jnp.where(incl, g_col - g_row, 0.0)), 0.0)
            x_mat = jnp.where(strict, -(b_col * kq_kt[:chunk] * decay), 0.0)
            x_diag = jnp.where(diag_blk, x_mat, 0.0)
            x_ref[slot(c, h)] = x_mat.astype(BF16)
            t_ref[slot(c, h)] = eye + x_diag
            p_ref[slot(c, h)] = x_diag.astype(BF16)
            attn_ref[slot(c, h)] = jnp.where(incl, kq_kt[chunk:] * decay, 0.0).astype(BF16)

    def square_round(c):
        for h in range(V_HEADS):
            xd = p_ref[slot(c, h)]
            p_ref[slot(c, h)] = _dot(xd, xd).astype(BF16)

    def base_round(c, j):
        for h in range(V_HEADS):
            pb = p_ref[slot(c, h)]
            tm = t_ref[slot(c, h)]
            if j < base_factors - 1:
                pt = _dot(pb, jnp.concatenate([pb, tm.astype(BF16)], axis=1))
                p_ref[slot(c, h)] = pt[:, :chunk].astype(BF16)
                t_ref[slot(c, h)] = tm + pt[:, chunk:]
            else:
                t_ref[slot(c, h)] = tm + _dot(pb, tm.astype(BF16))

    def widen_round(c, width):
        off = ((row // (2 * width)) == (col // (2 * width))) & ((row // width) != (col // width))
        for h in range(V_HEADS):
            tb16 = t_ref[slot(c, h)].astype(BF16)
            w_mat = _dot(jnp.where(off, x_ref[slot(c, h)], jnp.zeros((), BF16)), tb16)
            t_new = t_ref[slot(c, h)] + _dot(tb16, w_mat.astype(BF16))
            if 2 * width < chunk:
                t_ref[slot(c, h)] = t_new
            else:
                p_ref[slot(c, h)] = t_new.astype(BF16)

    def inverse_rounds(c):
        rounds = [functools.partial(square_round, c)]
        rounds += [functools.partial(base_round, c, j) for j in range(1, base_factors)]
        width = INV_BASE
        while width < chunk:
            rounds.append(functools.partial(widen_round, c, width))
            width *= 2
        return rounds

    for pr in range(K_HEADS):
        build_operands(0, pr)
    for c in range(n_chunks):
        pending = [(c + 1, pr) for pr in range(K_HEADS)] if c + 1 < n_chunks else []
        rounds = inverse_rounds(c)
        per_round = -(-len(pending) // len(rounds))
        for run_round in rounds:
            run_round()
            for args in pending[:per_round]:
                build_operands(*args)
            pending = pending[per_round:]

    for c in range(n_chunks):
        rs = slice(c * chunk, (c + 1) * chunk)
        for pr in range(K_HEADS):
            s_pair = jnp.concatenate([s_ref[0, 2 * pr], s_ref[0, 2 * pr + 1]], axis=1).astype(BF16)
            ks_ref[pr] = _dot(kq_of(c, pr), s_pair)
        for h in range(V_HEADS):
            es = slice((h % 2) * HEAD_DIM, (h % 2 + 1) * HEAD_DIM)
            b_col = col_ref[c, :, h:h + 1]
            gam = col_ref[c, :, 2 * V_HEADS + h:2 * V_HEADS + h + 1]
            v_h = v_ref[0, rs, h * HEAD_DIM:(h + 1) * HEAD_DIM].astype(F32)
            rhs = b_col * (v_h - gam * ks_ref[h // 2, :chunk, es])
            dl_ref[h] = _dot(p_ref[slot(c, h)], rhs.astype(BF16)).astype(BF16)
        for h in range(V_HEADS):
            s_decay = jnp.exp(g_ref[0, c, h:h + 1, chunk - 1:chunk])
            s_ref[0, h] = s_decay * s_ref[0, h] + _dot(kdt_ref[slot(c, h)], dl_ref[h])
        for h in range(V_HEADS):
            es = slice((h % 2) * HEAD_DIM, (h % 2 + 1) * HEAD_DIM)
            gam = col_ref[c, :, 2 * V_HEADS + h:2 * V_HEADS + h + 1]
            o = gam * ks_ref[h // 2, chunk:, es] + _dot(attn_ref[slot(c, h)], dl_ref[h])
            o_ref[0, rs, h * HEAD_DIM:(h + 1) * HEAD_DIM] = (o * _rms_scale(o) * won_ref[...]).astype(BF16)


def _gdn_core(q, k, v, beta_t, g_t, s0, w_onorm, *, chunk, tb):
    b, t_len, _ = q.shape
    nt = t_len // tb
    npc = tb // chunk
    tok = lambda width: pl.BlockSpec((1, tb, width), lambda i, j: (i, j, 0))
    rows = pl.BlockSpec((1, npc, V_HEADS, chunk), lambda i, j: (i, j, 0, 0))
    state = pl.BlockSpec((1, V_HEADS, HEAD_DIM, HEAD_DIM), lambda i, j: (i, 0, 0, 0))
    return pl.pallas_call(
        functools.partial(_gdn_core_kernel, chunk=chunk),
        grid=(b, nt),
        in_specs=[tok(D_QK), tok(D_QK), tok(D_V), rows, rows, state, _const_spec((1, HEAD_DIM))],
        out_specs=[tok(D_V), state],
        out_shape=[jax.ShapeDtypeStruct((b, t_len, D_V), BF16),
                   jax.ShapeDtypeStruct((b, V_HEADS, HEAD_DIM, HEAD_DIM), F32)],
        scratch_shapes=[pltpu.VMEM((npc * V_HEADS, chunk, chunk), F32),
                        pltpu.VMEM((npc * V_HEADS, chunk, chunk), BF16),
                        pltpu.VMEM((npc * V_HEADS, chunk, chunk), BF16),
                        pltpu.VMEM((npc * V_HEADS, chunk, chunk), BF16),
                        pltpu.VMEM((npc * V_HEADS, HEAD_DIM, chunk), BF16),
                        pltpu.VMEM((npc, chunk, 3 * V_HEADS), F32),
                        pltpu.VMEM((K_HEADS, 2 * chunk, 2 * HEAD_DIM), F32),
                        pltpu.VMEM((V_HEADS, chunk, HEAD_DIM), BF16)],
        compiler_params=_params(2),
        name="gdn_core",
    )(q, k, v, beta_t, g_t, s0, w_onorm.reshape(1, HEAD_DIM))


def _gdn_tail_kernel(o_ref, z_ref, x_ref, wout_ref, nw_ref, wg_ref, wu_ref, wd_ref, nf_ref,
                     y_ref, gated_ref, mid_ref, hb_ref, act_ref):
    for c in range(0, D_V, COL_BLOCK):
        cs = slice(c, c + COL_BLOCK)
        gated_ref[:, cs] = (o_ref[:, cs].astype(F32) * _silu(z_ref[:, cs].astype(F32))).astype(BF16)
    _residual_dot_store(x_ref, gated_ref, wout_ref, mid_ref)
    _ffn_store(mid_ref, nw_ref, wg_ref, wu_ref, wd_ref, hb_ref, act_ref, y_ref, nf_ref)


def _gdn_tail(o, z, x, w_out, norm_w, w_gate, w_up, w_down, norm_final, *, tb):
    n = x.shape[0]
    consts = (_mxu_weight(w_out), norm_w.reshape(1, D_MODEL), _mxu_weight(w_gate), _mxu_weight(w_up),
              _mxu_weight(w_down), norm_final.reshape(1, D_MODEL))
    return pl.pallas_call(
        _gdn_tail_kernel,
        grid=(n // tb,),
        in_specs=[pl.BlockSpec((tb, D_V), lambda i: (i, 0)), pl.BlockSpec((tb, D_V), lambda i: (i, 0)),
                  pl.BlockSpec((tb, D_MODEL), lambda i: (i, 0))] + [_const_spec(a.shape) for a in consts],
        out_specs=pl.BlockSpec((tb, D_MODEL), lambda i: (i, 0)),
        out_shape=jax.ShapeDtypeStruct((n, D_MODEL), F32),
        scratch_shapes=[pltpu.VMEM((tb, D_V), BF16), pltpu.VMEM((tb, D_MODEL), F32),
                        pltpu.VMEM((tb, D_MODEL), BF16), pltpu.VMEM((tb, D_FF), BF16)],
        compiler_params=_params(1),
        name="gdn_out_ffn",
    )(o, z, x, *consts)


def _trunk(x, conv_buf, s0, p, *, sgu_chunk, gdn_chunk, tb, gdn_tb, core_tb, write_v):
    b, t_len, _ = x.shape
    flat = lambda a: a.reshape(b * t_len, a.shape[-1])
    x0, v_rows = _sgu_layer(flat(x), p["norm_mix"][0], p["sgu_w_in"][0], p["sgu_ln_g"][0],
                            p["sgu_ln_b"][0], p["sgu_w_s"][0], p["sgu_b_s"][0], p["sgu_w_out"][0],
                            chunk=sgu_chunk, tb=tb, write_v=write_v)
    x1 = _ffn_layer(x0, p["norm_ffn"][0], p["ffn_w_gate"][0], p["ffn_w_up"][0], p["ffn_w_down"][0], tb=tb)
    q, k, v, z, beta_t, g_t, conv_tail = _gdn_proj(
        x1.reshape(b, t_len, D_MODEL), conv_buf, p["norm_mix"][1], p["gdn_w_in"][0],
        p["gdn_w_conv"][0], p["gdn_a_log"][0], p["gdn_dt_bias"][0], chunk=gdn_chunk, tb=gdn_tb)
    o, s_new = _gdn_core(q, k, v, beta_t, g_t, s0, p["gdn_w_onorm"][0], chunk=gdn_chunk, tb=core_tb)
    y = _gdn_tail(flat(o), flat(z), x1, p["gdn_w_out"][0], p["norm_ffn"][1], p["ffn_w_gate"][1],
                  p["ffn_w_up"][1], p["ffn_w_down"][1], p["norm_final"], tb=tb)
    conv_new = conv_tail[:, SUBLANES - (CONV_K - 1):, :]
    if write_v:
        v_rows = v_rows.reshape(b, t_len, D_SGU)
    return y.reshape(b, t_len, D_MODEL), s_new, conv_new, v_rows


def kernel(x_prompt, x_sample, state_gdn, state_conv, norm_mix, norm_ffn, norm_final, sgu_w_in, sgu_ln_g, sgu_ln_b, sgu_w_s, sgu_b_s, sgu_w_out, gdn_w_in, gdn_w_conv, gdn_a_log, gdn_dt_bias, gdn_w_onorm, gdn_w_out, ffn_w_gate, ffn_w_up, ffn_w_down):
    p = dict(norm_mix=norm_mix, norm_ffn=norm_ffn, norm_final=norm_final, sgu_w_in=sgu_w_in,
             sgu_ln_g=sgu_ln_g, sgu_ln_b=sgu_ln_b, sgu_w_s=sgu_w_s, sgu_b_s=sgu_b_s,
             sgu_w_out=sgu_w_out, gdn_w_in=gdn_w_in, gdn_w_conv=gdn_w_conv, gdn_a_log=gdn_a_log,
             gdn_dt_bias=gdn_dt_bias, gdn_w_onorm=gdn_w_onorm, gdn_w_out=gdn_w_out,
             ffn_w_gate=ffn_w_gate, ffn_w_up=ffn_w_up, ffn_w_down=ffn_w_down)
    bp = x_prompt.shape[0]
    dec_seq = x_sample.shape[1]
    zero_conv = jnp.zeros((bp, CONV_K - 1, D_QKV), F32)
    zero_state = jnp.zeros((bp, V_HEADS, HEAD_DIM, HEAD_DIM), F32)
    yp, sp, cp, _ = _trunk(x_prompt, zero_conv, zero_state, p, sgu_chunk=SGU_CHUNK,
                           gdn_chunk=2 * GDN_CHUNK, tb=256, gdn_tb=256, core_tb=256, write_v=False)
    ys, ss, cs, vs = _trunk(x_sample, state_conv[0], state_gdn[0], p, sgu_chunk=dec_seq,
                            gdn_chunk=dec_seq, tb=256, gdn_tb=dec_seq, core_tb=dec_seq, write_v=True)
    return (yp, ys, sp[None], cp[None], ss[None], cs[None], vs[None])
```

```python
import functools
import math

import jax
import jax.numpy as jnp
from jax import lax
from jax.experimental import pallas as pl
from jax.experimental.pallas import tpu as pltpu

D_MODEL = 1024
D_SGU = 2048
SGU_GROUPS = 8
SGU_GROUP_DIM = D_SGU // SGU_GROUPS
SGU_CHUNK = 128
HEAD_DIM = 128
K_HEADS = 8
V_HEADS = 16
D_QK = K_HEADS * HEAD_DIM
D_V = V_HEADS * HEAD_DIM
D_QKV = 2 * D_QK + D_V
CONV_K = 4
GDN_CHUNK = 64
D_FF = 2816
EPS = 1e-6
LN_EPS = 1e-5

VMEM_LIMIT_BYTES = 56 * 1024 * 1024
SUBLANES = 8
COL_BLOCK = 256
LANE_PAIR_TILE = 512
INV_BASE = 8

F32 = jnp.float32
BF16 = jnp.bfloat16
HIGHEST = lax.Precision.HIGHEST


def _const_spec(shape):
    nd = len(shape)
    return pl.BlockSpec(shape, lambda *_: (0,) * nd, pipeline_mode=pl.Buffered(1))


def _mxu_weight(w):
    wb = w.astype(BF16)
    if w.shape[-1] % LANE_PAIR_TILE == 0:
        wb = jnp.pad(wb, ((0, 0), (0, COL_BLOCK)))
    return wb


def _params(n_grid):
    return pltpu.CompilerParams(dimension_semantics=("arbitrary",) * n_grid,
                                vmem_limit_bytes=VMEM_LIMIT_BYTES)


def _rms_scale(x):
    return lax.rsqrt(jnp.mean(x * x, axis=-1, keepdims=True) + EPS)


def _silu(x):
    return x * jax.nn.sigmoid(x)


def _gelu_tanh(x):
    return 0.5 * x * (1.0 + jnp.tanh(math.sqrt(2.0 / math.pi) * (x + 0.044715 * (x * x * x))))


def _softplus(x):
    return jnp.maximum(x, 0.0) + jnp.log1p(jnp.exp(-jnp.abs(x)))


def _dot(a, b):
    return jnp.dot(a, b, preferred_element_type=F32)


def _dot_nt(a, b, precision=None):
    return lax.dot_general(a, b, (((1,), (1,)), ((), ())), precision=precision,
                           preferred_element_type=F32)


def _dot_tn(a, b):
    return lax.dot_general(a, b, (((0,), (0,)), ((), ())), preferred_element_type=F32)


def _residual_dot_store(x_ref, a_ref, w_ref, o_ref):
    ssq = jnp.zeros((x_ref.shape[0], 1), F32)
    for c in range(0, x_ref.shape[1], COL_BLOCK):
        cs = slice(c, c + COL_BLOCK)
        y = x_ref[:, cs] + _dot(a_ref[...], w_ref[:, cs])
        o_ref[:, cs] = y
        ssq = ssq + jnp.sum(y * y, axis=-1, keepdims=True)
    return ssq


def _run(steps):
    for step in steps:
        step()


def _interleave(a_steps, b_steps):
    keyed = [((i + 0.5) / len(a_steps), 0, s) for i, s in enumerate(a_steps)]
    keyed += [((j + 0.5) / len(b_steps), 1, s) for j, s in enumerate(b_steps)]
    return [s for _, _, s in sorted(keyed, key=lambda t: t[:2])]


def _ffn_steps(x_ref, nw_ref, wg_ref, wu_ref, wd_ref, hb_ref, act_ref, o_refs, nf_ref=None):
    def norm():
        x = x_ref[...]
        hb_ref[...] = (x * _rms_scale(x) * nw_ref[...]).astype(BF16)

    def gate_up(c):
        g = _dot(hb_ref[...], wg_ref[:, c:c + COL_BLOCK])
        u = _dot(hb_ref[...], wu_ref[:, c:c + COL_BLOCK])
        act_ref[:, c:c + COL_BLOCK] = (_silu(g) * u).astype(BF16)

    ssq = [jnp.zeros((x_ref.shape[0], 1), F32)]

    def down(c):
        cs = slice(c, c + COL_BLOCK)
        y = x_ref[:, cs] + _dot(act_ref[...], wd_ref[:, cs])
        for o_ref in o_refs:
            o_ref[:, cs] = y
        if nf_ref is not None:
            ssq[0] = ssq[0] + jnp.sum(y * y, axis=-1, keepdims=True)

    def final_norm():
        for o_ref in o_refs:
            o_ref[...] = o_ref[...] * lax.rsqrt(ssq[0] * (1.0 / D_MODEL) + EPS) * nf_ref[...]

    steps = [norm] + [functools.partial(gate_up, c) for c in range(0, D_FF, COL_BLOCK)]
    steps += [functools.partial(down, c) for c in range(0, D_MODEL, COL_BLOCK)]
    return steps + ([final_norm] if nf_ref is not None else [])


def _sgu_kernel(x_ref, nw_ref, win_ref, lng_ref, lnb_ref, ws_ref, bs_ref, wout_ref,
                *rest, chunk, write_v):
    if write_v:
        o_ref, v_ref, hb_ref, u_ref, vv_ref, gated_ref = rest
    else:
        o_ref, hb_ref, u_ref, vv_ref, gated_ref = rest
        v_ref = None
    tb = x_ref.shape[0]
    x = x_ref[...]
    hb_ref[...] = (x * _rms_scale(x) * nw_ref[...]).astype(BF16)

    vsum = jnp.zeros((tb, 1), F32)
    for c in range(0, 2 * D_SGU, COL_BLOCK):
        uv = _gelu_tanh(_dot(hb_ref[...], win_ref[:, c:c + COL_BLOCK]))
        if c < D_SGU:
            u_ref[:, c:c + COL_BLOCK] = uv
        else:
            vv_ref[:, c - D_SGU:c - D_SGU + COL_BLOCK] = uv
            vsum = vsum + jnp.sum(uv, axis=-1, keepdims=True)
    mu = vsum * (1.0 / D_SGU)
    vvar = jnp.zeros((tb, 1), F32)
    for c in range(0, D_SGU, COL_BLOCK):
        d = vv_ref[:, c:c + COL_BLOCK] - mu
        vvar = vvar + jnp.sum(d * d, axis=-1, keepdims=True)
    rstd = lax.rsqrt(vvar * (1.0 / D_SGU) + LN_EPS)

    row = lax.broadcasted_iota(jnp.int32, (chunk, chunk), 0)
    col = lax.broadcasted_iota(jnp.int32, (chunk, chunk), 1)
    causal = row >= col
    for g in range(SGU_GROUPS):
        c0 = g * SGU_GROUP_DIM
        cs = slice(c0, c0 + SGU_GROUP_DIM)
        vn = (vv_ref[:, cs] - mu) * rstd * lng_ref[:, cs] + lnb_ref[:, cs]
        if write_v:
            v_ref[:, cs] = vn
        vnb = vn.astype(BF16)
        wsg = jnp.where(causal, ws_ref[g], 0.0).astype(BF16)
        bias = bs_ref[:, g:g + 1]
        for r in range(0, tb, chunk):
            mixed = _dot(wsg, vnb[r:r + chunk]) + bias
            gated_ref[r:r + chunk, cs] = (u_ref[r:r + chunk, cs] * mixed).astype(BF16)
    _residual_dot_store(x_ref, gated_ref, wout_ref, o_ref)


def _sgu_layer(x, norm_w, w_in, ln_g, ln_b, w_s, b_s, w_out, *, chunk, tb, write_v):
    n = x.shape[0]
    ws = w_s[:, :chunk, :chunk]
    bs_t = b_s[:, :chunk].T
    out_shape = [jax.ShapeDtypeStruct((n, D_MODEL), F32)]
    out_specs = [pl.BlockSpec((tb, D_MODEL), lambda i: (i, 0))]
    if write_v:
        out_shape.append(jax.ShapeDtypeStruct((n, D_SGU), F32))
        out_specs.append(pl.BlockSpec((tb, D_SGU), lambda i: (i, 0)))
    consts = (norm_w.reshape(1, D_MODEL), _mxu_weight(w_in), ln_g.reshape(1, D_SGU),
              ln_b.reshape(1, D_SGU), ws, bs_t, _mxu_weight(w_out))
    res = pl.pallas_call(
        functools.partial(_sgu_kernel, chunk=chunk, write_v=write_v),
        grid=(n // tb,),
        in_specs=[pl.BlockSpec((tb, D_MODEL), lambda i: (i, 0))] + [_const_spec(a.shape) for a in consts],
        out_specs=out_specs,
        out_shape=out_shape,
        scratch_shapes=[
            pltpu.VMEM((tb, D_MODEL), BF16),
            pltpu.VMEM((tb, D_SGU), F32),
            pltpu.VMEM((tb, D_SGU), F32),
            pltpu.VMEM((tb, D_SGU), BF16),
        ],
        compiler_params=_params(1),
        name="sgu_mixer",
    )(x, *consts)
    return res if write_v else (res[0], None)


def _ffn_kernel(x_ref, nw_ref, wg_ref, wu_ref, wd_ref, o_ref, hb_ref, act_ref):
    _run(_ffn_steps(x_ref, nw_ref, wg_ref, wu_ref, wd_ref, hb_ref, act_ref, [o_ref]))


def _ffn_layer(x, norm_w, w_gate, w_up, w_down, *, tb):
    n = x.shape[0]
    consts = (norm_w.reshape(1, D_MODEL), _mxu_weight(w_gate), _mxu_weight(w_up), _mxu_weight(w_down))
    return pl.pallas_call(
        _ffn_kernel,
        grid=(n // tb,),
        in_specs=[pl.BlockSpec((tb, D_MODEL), lambda i: (i, 0))] + [_const_spec(a.shape) for a in consts],
        out_specs=pl.BlockSpec((tb, D_MODEL), lambda i: (i, 0)),
        out_shape=jax.ShapeDtypeStruct((n, D_MODEL), F32),
        scratch_shapes=[pltpu.VMEM((tb, D_MODEL), BF16), pltpu.VMEM((tb, D_FF), BF16)],
        compiler_params=_params(1),
        name="ffn",
    )(x, *consts)


def _proj_steps(x_ref, hb_ref, hist_ref, nw_ref, wqkv_ref, wz_ref, wbat_ref, wconv_ref, alog_ref, dtb_ref,
                q_ref, k_ref, v_ref, z_ref, beta_ref, g_ref, cnew_ref, *, chunk):
    tb = x_ref.shape[0]

    def norm():
        x = x_ref[...]
        hb_ref[...] = (x * _rms_scale(x) * nw_ref[...]).astype(BF16)

    def conv_block(c):
        cs = slice(c, c + COL_BLOCK)
        pre = _dot(hb_ref[...], wqkv_ref[:, cs])
        padded = jnp.concatenate([hist_ref[:, cs], pre], axis=0)
        hist_ref[:, cs] = pre[tb - SUBLANES:, :]
        cnew_ref[:, cs] = pre[tb - SUBLANES:, :]
        acc = padded * wconv_ref[0:1, cs]
        for i in range(1, CONV_K):
            acc = pltpu.roll(acc, 1, axis=0) + padded * wconv_ref[i:i + 1, cs]
        y = _silu(acc[SUBLANES:, :])
        if c < 2 * D_QK:
            dst, c_dst, scale = (q_ref, c, HEAD_DIM ** -0.5) if c < D_QK else (k_ref, c - D_QK, 1.0)
            for h in range(0, COL_BLOCK, HEAD_DIM):
                yh = y[:, h:h + HEAD_DIM]
                inv = lax.rsqrt(jnp.sum(yh * yh, axis=-1, keepdims=True) + EPS) * scale
                dst[:, c_dst + h:c_dst + h + HEAD_DIM] = (yh * inv).astype(BF16)
        else:
            v_ref[:, c - 2 * D_QK:c - 2 * D_QK + COL_BLOCK] = y
        blk = c // COL_BLOCK
        if blk % 2 == 1:
            cz = (blk // 2) * COL_BLOCK
            z_ref[:, cz:cz + COL_BLOCK] = _dot(hb_ref[...], wz_ref[:, cz:cz + COL_BLOCK])

    def gates():
        ba_t = _dot_nt(wbat_ref[...], hb_ref[...])
        beta_t = jax.nn.sigmoid(ba_t[:V_HEADS])
        loga_t = -jnp.exp(alog_ref[...]) * _softplus(ba_t[V_HEADS:] + dtb_ref[...])
        row = lax.broadcasted_iota(jnp.int32, (chunk, chunk), 0)
        col = lax.broadcasted_iota(jnp.int32, (chunk, chunk), 1)
        upper = (row <= col).astype(F32)
        for j in range(tb // chunk):
            js = slice(j * chunk, (j + 1) * chunk)
            beta_ref[j] = beta_t[:, js]
            g_ref[j] = jnp.dot(loga_t[:, js], upper, precision=HIGHEST, preferred_element_type=F32)

    return [norm] + [functools.partial(conv_block, c) for c in range(0, D_QKV, COL_BLOCK)] + [gates]


def _gdn_proj_kernel(x_ref, cbuf_ref, nw_ref, wqkv_ref, wz_ref, wbat_ref, wconv_ref,
                     alog_ref, dtb_ref,
                     q_ref, k_ref, v_ref, z_ref, beta_ref, g_ref, cnew_ref,
                     hb_ref, hist_ref, *, chunk):
    @pl.when(pl.program_id(1) == 0)
    def _():
        hist_ref[...] = cbuf_ref[0]

    _run(_proj_steps(x_ref.at[0], hb_ref, hist_ref, nw_ref, wqkv_ref, wz_ref, wbat_ref, wconv_ref,
                     alog_ref, dtb_ref, q_ref.at[0], k_ref.at[0], v_ref.at[0], z_ref.at[0],
                     beta_ref.at[0], g_ref.at[0], cnew_ref.at[0], chunk=chunk))


def _ffn_proj_kernel(x0_ref, cbuf_ref, nwf_ref, wg_ref, wu_ref, wd_ref,
                     nwp_ref, wqkv_ref, wz_ref, wbat_ref, wconv_ref, alog_ref, dtb_ref,
                     x1_ref, q_ref, k_ref, v_ref, z_ref, beta_ref, g_ref, cnew_ref,
                     x1s_ref, hbf_ref, act_ref, hbp_ref, hist_ref, *, chunk, tiles_per_stream):
    i = pl.program_id(0)

    @pl.when(i == 0)
    def _():
        x1s_ref[...] = jnp.zeros_like(x1s_ref)
        hist_ref[...] = jnp.zeros_like(hist_ref)

    @pl.when((i >= 1) & ((i - 1) % tiles_per_stream == 0))
    def _():
        hist_ref[...] = cbuf_ref[0]

    ffn = _ffn_steps(x0_ref, nwf_ref, wg_ref, wu_ref, wd_ref, hbf_ref, act_ref, [x1_ref, x1s_ref])
    proj = _proj_steps(x1s_ref, hbp_ref, hist_ref, nwp_ref, wqkv_ref, wz_ref, wbat_ref, wconv_ref,
                       alog_ref, dtb_ref, q_ref, k_ref, v_ref, z_ref, beta_ref, g_ref, cnew_ref.at[0],
                       chunk=chunk)
    proj[0]()
    _run(_interleave(ffn, proj[1:]))


def _gdn_proj(x, conv_buf, norm_w, w_in, w_conv, a_log, dt_bias, *, chunk, tb):
    b, t_len, _ = x.shape
    nt = t_len // tb
    npc = tb // chunk
    cbuf = jnp.pad(conv_buf, ((0, 0), (SUBLANES - (CONV_K - 1), 0), (0, 0)))
    w_ba_t = w_in[:, D_QKV + D_V:].T.astype(BF16)
    consts = (norm_w.reshape(1, D_MODEL), _mxu_weight(w_in[:, :D_QKV]),
              _mxu_weight(w_in[:, D_QKV:D_QKV + D_V]), w_ba_t, w_conv,
              a_log.reshape(V_HEADS, 1), dt_bias.reshape(V_HEADS, 1))
    tok = lambda width: pl.BlockSpec((1, tb, width), lambda i, j: (i, j, 0))
    rows = pl.BlockSpec((1, npc, V_HEADS, chunk), lambda i, j: (i, j, 0, 0))
    return pl.pallas_call(
        functools.partial(_gdn_proj_kernel, chunk=chunk),
        grid=(b, nt),
        in_specs=[tok(D_MODEL), pl.BlockSpec((1, SUBLANES, D_QKV), lambda i, j: (i, 0, 0))]
        + [_const_spec(a.shape) for a in consts],
        out_specs=[tok(D_QK), tok(D_QK), tok(D_V), tok(D_V), rows, rows,
                   pl.BlockSpec((1, SUBLANES, D_QKV), lambda i, j: (i, 0, 0))],
        out_shape=[
            jax.ShapeDtypeStruct((b, t_len, D_QK), BF16),
            jax.ShapeDtypeStruct((b, t_len, D_QK), BF16),
            jax.ShapeDtypeStruct((b, t_len, D_V), F32),
            jax.ShapeDtypeStruct((b, t_len, D_V), F32),
            jax.ShapeDtypeStruct((b, t_len // chunk, V_HEADS, chunk), F32),
            jax.ShapeDtypeStruct((b, t_len // chunk, V_HEADS, chunk), F32),
            jax.ShapeDtypeStruct((b, SUBLANES, D_QKV), F32),
        ],
        scratch_shapes=[pltpu.VMEM((tb, D_MODEL), BF16),
                        pltpu.VMEM((SUBLANES, D_QKV), F32)],
        compiler_params=_params(2),
        name="gdn_proj",
    )(x, cbuf, *consts)


def _ffn_proj(x0, conv_buf, ffn_norm_w, w_gate, w_up, w_down, norm_w, w_in, w_conv, a_log, dt_bias,
              *, t_len, chunk, tb):
    n = x0.shape[0]
    n_tiles = n // tb
    tiles_per_stream = t_len // tb
    npc = tb // chunk
    cbuf = jnp.pad(conv_buf, ((0, 0), (SUBLANES - (CONV_K - 1), 0), (0, 0)))
    w_ba_t = w_in[:, D_QKV + D_V:].T.astype(BF16)
    consts = (ffn_norm_w.reshape(1, D_MODEL), _mxu_weight(w_gate), _mxu_weight(w_up), _mxu_weight(w_down),
              norm_w.reshape(1, D_MODEL), _mxu_weight(w_in[:, :D_QKV]),
              _mxu_weight(w_in[:, D_QKV:D_QKV + D_V]), w_ba_t, w_conv,
              a_log.reshape(V_HEADS, 1), dt_bias.reshape(V_HEADS, 1))
    ffn_tile = lambda i: jnp.minimum(i, n_tiles - 1)
    proj_tile = lambda i: jnp.maximum(i - 1, 0)
    tok = lambda width: pl.BlockSpec((tb, width), lambda i: (proj_tile(i), 0))
    rows = pl.BlockSpec((npc, V_HEADS, chunk), lambda i: (proj_tile(i), 0, 0))
    conv_state = pl.BlockSpec((1, SUBLANES, D_QKV), lambda i: (proj_tile(i) // tiles_per_stream, 0, 0))
    return pl.pallas_call(
        functools.partial(_ffn_proj_kernel, chunk=chunk, tiles_per_stream=tiles_per_stream),
        grid=(n_tiles + 1,),
        in_specs=[pl.BlockSpec((tb, D_MODEL), lambda i: (ffn_tile(i), 0)), conv_state]
        + [_const_spec(a.shape) for a in consts],
        out_specs=[pl.BlockSpec((tb, D_MODEL), lambda i: (ffn_tile(i), 0)),
                   tok(D_QK), tok(D_QK), tok(D_V), tok(D_V), rows, rows, conv_state],
        out_shape=[
            jax.ShapeDtypeStruct((n, D_MODEL), F32),
            jax.ShapeDtypeStruct((n, D_QK), BF16),
            jax.ShapeDtypeStruct((n, D_QK), BF16),
            jax.ShapeDtypeStruct((n, D_V), F32),
            jax.ShapeDtypeStruct((n, D_V), F32),
            jax.ShapeDtypeStruct((n // chunk, V_HEADS, chunk), F32),
            jax.ShapeDtypeStruct((n // chunk, V_HEADS, chunk), F32),
            jax.ShapeDtypeStruct((n // t_len, SUBLANES, D_QKV), F32),
        ],
        scratch_shapes=[pltpu.VMEM((tb, D_MODEL), F32), pltpu.VMEM((tb, D_MODEL), BF16),
                        pltpu.VMEM((tb, D_FF), BF16), pltpu.VMEM((tb, D_MODEL), BF16),
                        pltpu.VMEM((SUBLANES, D_QKV), F32)],
        compiler_params=_params(1),
        name="ffn_gdn_proj",
    )(x0, cbuf, *consts)


def _gdn_core_kernel(q_ref, k_ref, v_ref, beta_ref, g_ref, s0_ref, won_ref, o_ref, s_ref,
                     t_ref, p_ref, x_ref, attn_ref, kdt_ref, col_ref, ks_ref, dl_ref, *, chunk):
    @pl.when(pl.program_id(1) == 0)
    def _():
        s_ref[...] = s0_ref[...]

    n_chunks = q_ref.shape[1] // chunk
    row = lax.broadcasted_iota(jnp.int32, (chunk, chunk), 0)
    col = lax.broadcasted_iota(jnp.int32, (chunk, chunk), 1)
    incl = row >= col
    strict = row > col
    eye = (row == col).astype(F32)
    diag_blk = (row // INV_BASE) == (col // INV_BASE)
    base_factors = INV_BASE.bit_length() - 1
    slot = lambda c, h: c * V_HEADS + h

    def kq_of(c, pr):
        rs = slice(c * chunk, (c + 1) * chunk)
        hs = slice(pr * HEAD_DIM, (pr + 1) * HEAD_DIM)
        return jnp.concatenate([k_ref[0, rs, hs], q_ref[0, rs, hs]], axis=0)

    for c in range(n_chunks):
        g_rows = g_ref[0, c]
        rows_t = jnp.concatenate([beta_ref[0, c], g_rows, jnp.exp(g_rows)], axis=0)
        col_ref[c] = rows_t.T

    def build_operands(c, pr):
        kq = kq_of(c, pr)
        kq_kt = _dot_nt(kq, kq[:chunk])
        k_t = kq[:chunk].astype(F32).T
        for h in (2 * pr, 2 * pr + 1):
            g_row = g_ref[0, c, h:h + 1, :]
            k_dec = jnp.exp(g_row[:, chunk - 1:chunk] - g_row)
            kdt_ref[slot(c, h)] = (k_t * k_dec).astype(BF16)
            b_col = col_ref[c, :, h:h + 1]
            g_col = col_ref[c, :, V_HEADS + h:V_HEADS + h + 1]
            decay = jnp.where(incl, jnp.exp(jnp.where(incl, g_col - g_row, 0.0)), 0.0)
            x_mat = jnp.where(strict, -(b_col * kq_kt[:chunk] * decay), 0.0)
            x_diag = jnp.where(diag_blk, x_mat, 0.0)
            x_ref[slot(c, h)] = x_mat.astype(BF16)
            t_ref[slot(c, h)] = eye + x_diag
            p_ref[slot(c, h)] = x_diag.astype(BF16)
            attn_ref[slot(c, h)] = jnp.where(incl, kq_kt[chunk:] * decay, 0.0).astype(BF16)

    def square_round(c):
        for h in range(V_HEADS):
            xd = p_ref[slot(c, h)]
            p_ref[slot(c, h)] = _dot(xd, xd).astype(BF16)

    def base_round(c, j):
        for h in range(V_HEADS):
            pb = p_ref[slot(c, h)]
            tm = t_ref[slot(c, h)]
            if j < base_factors - 1:
                pt = _dot(pb, jnp.concatenate([pb, tm.astype(BF16)], axis=1))
                p_ref[slot(c, h)] = pt[:, :chunk].astype(BF16)
                t_ref[slot(c, h)] = tm + pt[:, chunk:]
            else:
                t_ref[slot(c, h)] = tm + _dot(pb, tm.astype(BF16))

    def widen_round(c, width):
        off = ((row // (2 * width)) == (col // (2 * width))) & ((row // width) != (col // width))
        for h in range(V_HEADS):
            tb16 = t_ref[slot(c, h)].astype(BF16)
            w_mat = _dot(jnp.where(off, x_ref[slot(c, h)], jnp.zeros((), BF16)), tb16)
            t_new = t_ref[slot(c, h)] + _dot(tb16, w_mat.astype(BF16))
            if 2 * width < chunk:
                t_ref[slot(c, h)] = t_new
            else:
                p_ref[slot(c, h)] = t_new.astype(BF16)

    def inverse_rounds(c):
        rounds = [functools.partial(square_round, c)]
        rounds += [functools.partial(base_round, c, j) for j in range(1, base_factors)]
        width = INV_BASE
        while width < chunk:
            rounds.append(functools.partial(widen_round, c, width))
            width *= 2
        return rounds

    for pr in range(K_HEADS):
        build_operands(0, pr)
    for c in range(n_chunks):
        pending = [(c + 1, pr) for pr in range(K_HEADS)] if c + 1 < n_chunks else []
        rounds = inverse_rounds(c)
        per_round = -(-len(pending) // len(rounds))
        for run_round in rounds:
            run_round()
            for args in pending[:per_round]:
                build_operands(*args)
            pending = pending[per_round:]

    for c in range(n_chunks):
        rs = slice(c * chunk, (c + 1) * chunk)
        for pr in range(K_HEADS):
            s_pair = jnp.concatenate([s_ref[0, 2 * pr], s_ref[0, 2 * pr + 1]], axis=1).astype(BF16)
            ks_ref[pr] = _dot(kq_of(c, pr), s_pair)
        for h in range(V_HEADS):
            es = slice((h % 2) * HEAD_DIM, (h % 2 + 1) * HEAD_DIM)
            b_col = col_ref[c, :, h:h + 1]
            gam = col_ref[c, :, 2 * V_HEADS + h:2 * V_HEADS + h + 1]
            rhs = b_col * (v_ref[0, rs, h * HEAD_DIM:(h + 1) * HEAD_DIM] - gam * ks_ref[h // 2, :chunk, es])
            dl_ref[h] = _dot(p_ref[slot(c, h)], rhs.astype(BF16)).astype(BF16)
        for h in range(V_HEADS):
            s_decay = jnp.exp(g_ref[0, c, h:h + 1, chunk - 1:chunk])
            s_ref[0, h] = s_decay * s_ref[0, h] + _dot(kdt_ref[slot(c, h)], dl_ref[h])
        for h in range(V_HEADS):
            es = slice((h % 2) * HEAD_DIM, (h % 2 + 1) * HEAD_DIM)
            gam = col_ref[c, :, 2 * V_HEADS + h:2 * V_HEADS + h + 1]
            o = gam * ks_ref[h // 2, chunk:, es] + _dot(attn_ref[slot(c, h)], dl_ref[h])
            o_ref[0, rs, h * HEAD_DIM:(h + 1) * HEAD_DIM] = o * _rms_scale(o) * won_ref[...]


def _gdn_core(q, k, v, beta_t, g_t, s0, w_onorm, *, chunk, tb):
    b, t_len, _ = q.shape
    nt = t_len // tb
    npc = tb // chunk
    tok = lambda width: pl.BlockSpec((1, tb, width), lambda i, j: (i, j, 0))
    rows = pl.BlockSpec((1, npc, V_HEADS, chunk), lambda i, j: (i, j, 0, 0))
    state = pl.BlockSpec((1, V_HEADS, HEAD_DIM, HEAD_DIM), lambda i, j: (i, 0, 0, 0))
    return pl.pallas_call(
        functools.partial(_gdn_core_kernel, chunk=chunk),
        grid=(b, nt),
        in_specs=[tok(D_QK), tok(D_QK), tok(D_V), rows, rows, state, _const_spec((1, HEAD_DIM))],
        out_specs=[tok(D_V), state],
        out_shape=[jax.ShapeDtypeStruct((b, t_len, D_V), F32),
                   jax.ShapeDtypeStruct((b, V_HEADS, HEAD_DIM, HEAD_DIM), F32)],
        scratch_shapes=[pltpu.VMEM((npc * V_HEADS, chunk, chunk), F32),
                        pltpu.VMEM((npc * V_HEADS, chunk, chunk), BF16),
                        pltpu.VMEM((npc * V_HEADS, chunk, chunk), BF16),
                        pltpu.VMEM((npc * V_HEADS, chunk, chunk), BF16),
                        pltpu.VMEM((npc * V_HEADS, HEAD_DIM, chunk), BF16),
                        pltpu.VMEM((npc, chunk, 3 * V_HEADS), F32),
                        pltpu.VMEM((K_HEADS, 2 * chunk, 2 * HEAD_DIM), F32),
                        pltpu.VMEM((V_HEADS, chunk, HEAD_DIM), BF16)],
        compiler_params=_params(2),
        name="gdn_core",
    )(q, k, v, beta_t, g_t, s0, w_onorm.reshape(1, HEAD_DIM))


def _gdn_tail_kernel(o_ref, z_ref, x_ref, wout_ref, nw_ref, wg_ref, wu_ref, wd_ref, nf_ref,
                     y_ref, gated_ref, mid_ref, hb_ref, act_ref):
    for c in range(0, D_V, COL_BLOCK):
        cs = slice(c, c + COL_BLOCK)
        gated_ref[:, cs] = (o_ref[:, cs] * _silu(z_ref[:, cs])).astype(BF16)
    _residual_dot_store(x_ref, gated_ref, wout_ref, mid_ref)
    _run(_ffn_steps(mid_ref, nw_ref, wg_ref, wu_ref, wd_ref, hb_ref, act_ref, [y_ref], nf_ref))


def _gdn_tail(o, z, x, w_out, norm_w, w_gate, w_up, w_down, norm_final, *, tb):
    n = x.shape[0]
    consts = (_mxu_weight(w_out), norm_w.reshape(1, D_MODEL), _mxu_weight(w_gate), _mxu_weight(w_up),
              _mxu_weight(w_down), norm_final.reshape(1, D_MODEL))
    return pl.pallas_call(
        _gdn_tail_kernel,
        grid=(n // tb,),
        in_specs=[pl.BlockSpec((tb, D_V), lambda i: (i, 0)), pl.BlockSpec((tb, D_V), lambda i: (i, 0)),
                  pl.BlockSpec((tb, D_MODEL), lambda i: (i, 0))] + [_const_spec(a.shape) for a in consts],
        out_specs=pl.BlockSpec((tb, D_MODEL), lambda i: (i, 0)),
        out_shape=jax.ShapeDtypeStruct((n, D_MODEL), F32),
        scratch_shapes=[pltpu.VMEM((tb, D_V), BF16), pltpu.VMEM((tb, D_MODEL), F32),
                        pltpu.VMEM((tb, D_MODEL), BF16), pltpu.VMEM((tb, D_FF), BF16)],
        compiler_params=_params(1),
        name="gdn_out_ffn",
    )(o, z, x, *consts)


def _trunk(x, conv_buf, s0, p, *, sgu_chunk, gdn_chunk, tb, gdn_tb, core_tb, write_v):
    b, t_len, _ = x.shape
    flat = lambda a: a.reshape(b * t_len, a.shape[-1])
    x0, v_rows = _sgu_layer(flat(x), p["norm_mix"][0], p["sgu_w_in"][0], p["sgu_ln_g"][0],
                            p["sgu_ln_b"][0], p["sgu_w_s"][0], p["sgu_b_s"][0], p["sgu_w_out"][0],
                            chunk=sgu_chunk, tb=tb, write_v=write_v)
    ffn_w = (p["norm_ffn"][0], p["ffn_w_gate"][0], p["ffn_w_up"][0], p["ffn_w_down"][0])
    proj_w = (p["norm_mix"][1], p["gdn_w_in"][0], p["gdn_w_conv"][0], p["gdn_a_log"][0], p["gdn_dt_bias"][0])
    if t_len > gdn_tb:
        x1, *proj = _ffn_proj(x0, conv_buf, *ffn_w, *proj_w, t_len=t_len, chunk=gdn_chunk, tb=gdn_tb)
        q, k, v, z, beta_t, g_t = (a.reshape(b, a.shape[0] // b, *a.shape[1:]) for a in proj[:-1])
        conv_tail = proj[-1]
    else:
        x1 = _ffn_layer(x0, *ffn_w, tb=tb)
        q, k, v, z, beta_t, g_t, conv_tail = _gdn_proj(x1.reshape(b, t_len, D_MODEL), conv_buf, *proj_w,
                                                       chunk=gdn_chunk, tb=gdn_tb)
    o, s_new = _gdn_core(q, k, v, beta_t, g_t, s0, p["gdn_w_onorm"][0], chunk=gdn_chunk, tb=core_tb)
    y = _gdn_tail(flat(o), flat(z), x1, p["gdn_w_out"][0], p["norm_ffn"][1], p["ffn_w_gate"][1],
                  p["ffn_w_up"][1], p["ffn_w_down"][1], p["norm_final"], tb=tb)
    conv_new = conv_tail[:, SUBLANES - (CONV_K - 1):, :]
    if write_v:
        v_rows = v_rows.reshape(b, t_len, D_SGU)
    return y.reshape(b, t_len, D_MODEL), s_new, conv_new, v_rows


def kernel(x_prompt, x_sample, state_gdn, state_conv, norm_mix, norm_ffn, norm_final, sgu_w_in, sgu_ln_g, sgu_ln_b, sgu_w_s, sgu_b_s, sgu_w_out, gdn_w_in, gdn_w_conv, gdn_a_log, gdn_dt_bias, gdn_w_onorm, gdn_w_out, ffn_w_gate, ffn_w_up, ffn_w_down):
    p = dict(norm_mix=norm_mix, norm_ffn=norm_ffn, norm_final=norm_final, sgu_w_in=sgu_w_in,
             sgu_ln_g=sgu_ln_g, sgu_ln_b=sgu_ln_b, sgu_w_s=sgu_w_s, sgu_b_s=sgu_b_s,
             sgu_w_out=sgu_w_out, gdn_w_in=gdn_w_in, gdn_w_conv=gdn_w_conv, gdn_a_log=gdn_a_log,
             gdn_dt_bias=gdn_dt_bias, gdn_w_onorm=gdn_w_onorm, gdn_w_out=gdn_w_out,
             ffn_w_gate=ffn_w_gate, ffn_w_up=ffn_w_up, ffn_w_down=ffn_w_down)
    bp = x_prompt.shape[0]
    dec_seq = x_sample.shape[1]
    zero_conv = jnp.zeros((bp, CONV_K - 1, D_QKV), F32)
    zero_state = jnp.zeros((bp, V_HEADS, HEAD_DIM, HEAD_DIM), F32)
    yp, sp, cp, _ = _trunk(x_prompt, zero_conv, zero_state, p, sgu_chunk=SGU_CHUNK,
                           gdn_chunk=2 * GDN_CHUNK, tb=256, gdn_tb=256, core_tb=256, write_v=False)
    ys, ss, cs, vs = _trunk(x_sample, state_conv[0], state_gdn[0], p, sgu_chunk=dec_seq,
                            gdn_chunk=dec_seq, tb=256, gdn_tb=dec_seq, core_tb=dec_seq, write_v=True)
    return (yp, ys, sp[None], cp[None], ss[None], cs[None], vs[None])
```

```python
import functools
import math

import jax
import jax.numpy as jnp
from jax import lax
from jax.experimental import pallas as pl
from jax.experimental.pallas import tpu as pltpu

D_MODEL = 1024
D_SGU = 2048
SGU_GROUPS = 8
SGU_GROUP_DIM = D_SGU // SGU_GROUPS
SGU_CHUNK = 128
HEAD_DIM = 128
K_HEADS = 8
V_HEADS = 16
D_QK = K_HEADS * HEAD_DIM
D_V = V_HEADS * HEAD_DIM
D_QKV = 2 * D_QK + D_V
CONV_K = 4
GDN_CHUNK = 64
D_FF = 2816
EPS = 1e-6
LN_EPS = 1e-5

VMEM_LIMIT_BYTES = 56 * 1024 * 1024
SUBLANES = 8
COL_BLOCK = 256
LANE_PAIR_TILE = 512
INV_BASE = 8

F32 = jnp.float32
BF16 = jnp.bfloat16
HIGHEST = lax.Precision.HIGHEST


def _const_spec(shape):
    nd = len(shape)
    return pl.BlockSpec(shape, lambda *_: (0,) * nd, pipeline_mode=pl.Buffered(1))


def _mxu_weight(w):
    wb = w.astype(BF16)
    if w.shape[-1] % LANE_PAIR_TILE == 0:
        wb = jnp.pad(wb, ((0, 0), (0, COL_BLOCK)))
    return wb


def _params(n_grid):
    return pltpu.CompilerParams(dimension_semantics=("arbitrary",) * n_grid,
                                vmem_limit_bytes=VMEM_LIMIT_BYTES)


def _rms_scale(x):
    return lax.rsqrt(jnp.mean(x * x, axis=-1, keepdims=True) + EPS)


def _silu(x):
    return x * jax.nn.sigmoid(x)


def _gelu_tanh(x):
    return 0.5 * x * (1.0 + jnp.tanh(math.sqrt(2.0 / math.pi) * (x + 0.044715 * (x * x * x))))


def _softplus(x):
    return jnp.maximum(x, 0.0) + jnp.log1p(jnp.exp(-jnp.abs(x)))


def _dot(a, b):
    return jnp.dot(a, b, preferred_element_type=F32)


def _dot_nt(a, b, precision=None):
    return lax.dot_general(a, b, (((1,), (1,)), ((), ())), precision=precision,
                           preferred_element_type=F32)


def _dot_tn(a, b):
    return lax.dot_general(a, b, (((0,), (0,)), ((), ())), preferred_element_type=F32)


def _residual_dot_store(x_ref, a_ref, w_ref, o_ref):
    ssq = jnp.zeros((x_ref.shape[0], 1), F32)
    for c in range(0, x_ref.shape[1], COL_BLOCK):
        cs = slice(c, c + COL_BLOCK)
        y = x_ref[:, cs] + _dot(a_ref[...], w_ref[:, cs])
        o_ref[:, cs] = y
        ssq = ssq + jnp.sum(y * y, axis=-1, keepdims=True)
    return ssq


def _run(steps):
    for step in steps:
        step()


def _interleave(a_steps, b_steps):
    keyed = [((i + 0.5) / len(a_steps), 0, s) for i, s in enumerate(a_steps)]
    keyed += [((j + 0.5) / len(b_steps), 1, s) for j, s in enumerate(b_steps)]
    return [s for _, _, s in sorted(keyed, key=lambda t: t[:2])]


def _ffn_steps(x_ref, nw_ref, wg_ref, wu_ref, wd_ref, hb_ref, act_ref, o_refs, nf_ref=None):
    def norm():
        x = x_ref[...]
        hb_ref[...] = (x * _rms_scale(x) * nw_ref[...]).astype(BF16)

    def gate_up(c):
        g = _dot(hb_ref[...], wg_ref[:, c:c + COL_BLOCK])
        u = _dot(hb_ref[...], wu_ref[:, c:c + COL_BLOCK])
        act_ref[:, c:c + COL_BLOCK] = (_silu(g) * u).astype(BF16)

    ssq = [jnp.zeros((x_ref.shape[0], 1), F32)]

    def down(c):
        cs = slice(c, c + COL_BLOCK)
        y = x_ref[:, cs] + _dot(act_ref[...], wd_ref[:, cs])
        for o_ref in o_refs:
            o_ref[:, cs] = y
        if nf_ref is not None:
            ssq[0] = ssq[0] + jnp.sum(y * y, axis=-1, keepdims=True)

    def final_norm():
        for o_ref in o_refs:
            o_ref[...] = o_ref[...] * lax.rsqrt(ssq[0] * (1.0 / D_MODEL) + EPS) * nf_ref[...]

    steps = [norm] + [functools.partial(gate_up, c) for c in range(0, D_FF, COL_BLOCK)]
    steps += [functools.partial(down, c) for c in range(0, D_MODEL, COL_BLOCK)]
    return steps + ([final_norm] if nf_ref is not None else [])


def _sgu_kernel(x_ref, nw_ref, win_ref, lng_ref, lnb_ref, ws_ref, bs_ref, wout_ref,
                *rest, chunk, write_v):
    if write_v:
        o_ref, v_ref, hb_ref, u_ref, vv_ref, gated_ref = rest
    else:
        o_ref, hb_ref, u_ref, vv_ref, gated_ref = rest
        v_ref = None
    tb = x_ref.shape[0]
    x = x_ref[...]
    hb_ref[...] = (x * _rms_scale(x) * nw_ref[...]).astype(BF16)

    vsum = jnp.zeros((tb, 1), F32)
    for c in range(0, 2 * D_SGU, COL_BLOCK):
        uv = _gelu_tanh(_dot(hb_ref[...], win_ref[:, c:c + COL_BLOCK]))
        if c < D_SGU:
            u_ref[:, c:c + COL_BLOCK] = uv
        else:
            vv_ref[:, c - D_SGU:c - D_SGU + COL_BLOCK] = uv
            vsum = vsum + jnp.sum(uv, axis=-1, keepdims=True)
    mu = vsum * (1.0 / D_SGU)
    vvar = jnp.zeros((tb, 1), F32)
    for c in range(0, D_SGU, COL_BLOCK):
        d = vv_ref[:, c:c + COL_BLOCK] - mu
        vvar = vvar + jnp.sum(d * d, axis=-1, keepdims=True)
    rstd = lax.rsqrt(vvar * (1.0 / D_SGU) + LN_EPS)

    row = lax.broadcasted_iota(jnp.int32, (chunk, chunk), 0)
    col = lax.broadcasted_iota(jnp.int32, (chunk, chunk), 1)
    causal = row >= col
    for g in range(SGU_GROUPS):
        c0 = g * SGU_GROUP_DIM
        cs = slice(c0, c0 + SGU_GROUP_DIM)
        vn = (vv_ref[:, cs] - mu) * rstd * lng_ref[:, cs] + lnb_ref[:, cs]
        if write_v:
            v_ref[:, cs] = vn
        vnb = vn.astype(BF16)
        wsg = jnp.where(causal, ws_ref[g], 0.0).astype(BF16)
        bias = bs_ref[:, g:g + 1]
        for r in range(0, tb, chunk):
            mixed = _dot(wsg, vnb[r:r + chunk]) + bias
            gated_ref[r:r + chunk, cs] = (u_ref[r:r + chunk, cs] * mixed).astype(BF16)
    _residual_dot_store(x_ref, gated_ref, wout_ref, o_ref)


def _sgu_layer(x, norm_w, w_in, ln_g, ln_b, w_s, b_s, w_out, *, chunk, tb, write_v):
    n = x.shape[0]
    ws = w_s[:, :chunk, :chunk]
    bs_t = b_s[:, :chunk].T
    out_shape = [jax.ShapeDtypeStruct((n, D_MODEL), F32)]
    out_specs = [pl.BlockSpec((tb, D_MODEL), lambda i: (i, 0))]
    if write_v:
        out_shape.append(jax.ShapeDtypeStruct((n, D_SGU), F32))
        out_specs.append(pl.BlockSpec((tb, D_SGU), lambda i: (i, 0)))
    consts = (norm_w.reshape(1, D_MODEL), _mxu_weight(w_in), ln_g.reshape(1, D_SGU),
              ln_b.reshape(1, D_SGU), ws, bs_t, _mxu_weight(w_out))
    res = pl.pallas_call(
        functools.partial(_sgu_kernel, chunk=chunk, write_v=write_v),
        grid=(n // tb,),
        in_specs=[pl.BlockSpec((tb, D_MODEL), lambda i: (i, 0))] + [_const_spec(a.shape) for a in consts],
        out_specs=out_specs,
        out_shape=out_shape,
        scratch_shapes=[
            pltpu.VMEM((tb, D_MODEL), BF16),
            pltpu.VMEM((tb, D_SGU), F32),
            pltpu.VMEM((tb, D_SGU), F32),
            pltpu.VMEM((tb, D_SGU), BF16),
        ],
        compiler_params=_params(1),
        name="sgu_mixer",
    )(x, *consts)
    return res if write_v else (res[0], None)


def _ffn_kernel(x_ref, nw_ref, wg_ref, wu_ref, wd_ref, o_ref, hb_ref, act_ref):
    _run(_ffn_steps(x_ref, nw_ref, wg_ref, wu_ref, wd_ref, hb_ref, act_ref, [o_ref]))


def _ffn_layer(x, norm_w, w_gate, w_up, w_down, *, tb):
    n = x.shape[0]
    consts = (norm_w.reshape(1, D_MODEL), _mxu_weight(w_gate), _mxu_weight(w_up), _mxu_weight(w_down))
    return pl.pallas_call(
        _ffn_kernel,
        grid=(n // tb,),
        in_specs=[pl.BlockSpec((tb, D_MODEL), lambda i: (i, 0))] + [_const_spec(a.shape) for a in consts],
        out_specs=pl.BlockSpec((tb, D_MODEL), lambda i: (i, 0)),
        out_shape=jax.ShapeDtypeStruct((n, D_MODEL), F32),
        scratch_shapes=[pltpu.VMEM((tb, D_MODEL), BF16), pltpu.VMEM((tb, D_FF), BF16)],
        compiler_params=_params(1),
        name="ffn",
    )(x, *consts)


def _proj_steps(x_ref, hb_ref, hist_refs, carry, nw_ref, wqkv_ref, wz_ref, wbat_ref, wconv_ref, alog_ref,
                dtb_ref, q_ref, k_ref, v_ref, z_ref, beta_ref, g_ref, cnew_refs, *, chunk):
    tb = x_ref.shape[0]
    seg = tb // len(hist_refs)

    def norm():
        x = x_ref[...]
        hb_ref[...] = (x * _rms_scale(x) * nw_ref[...]).astype(BF16)

    def conv_block(c):
        cs = slice(c, c + COL_BLOCK)
        pre_all = _dot(hb_ref[...], wqkv_ref[:, cs])
        for s, (hist_ref, cnew_ref) in enumerate(zip(hist_refs, cnew_refs)):
            rs = slice(s * seg, (s + 1) * seg)
            pre = pre_all[rs]
            padded = jnp.concatenate([hist_ref[:, cs], pre], axis=0)
            if carry:
                hist_ref[:, cs] = pre[seg - SUBLANES:, :]
            cnew_ref[:, cs] = pre[seg - SUBLANES:, :]
            acc = padded * wconv_ref[0:1, cs]
            for i in range(1, CONV_K):
                acc = pltpu.roll(acc, 1, axis=0) + padded * wconv_ref[i:i + 1, cs]
            y = _silu(acc[SUBLANES:, :])
            if c < 2 * D_QK:
                dst, c_dst, scale = (q_ref, c, HEAD_DIM ** -0.5) if c < D_QK else (k_ref, c - D_QK, 1.0)
                for h in range(0, COL_BLOCK, HEAD_DIM):
                    yh = y[:, h:h + HEAD_DIM]
                    inv = lax.rsqrt(jnp.sum(yh * yh, axis=-1, keepdims=True) + EPS) * scale
                    dst[rs, c_dst + h:c_dst + h + HEAD_DIM] = (yh * inv).astype(BF16)
            else:
                v_ref[rs, c - 2 * D_QK:c - 2 * D_QK + COL_BLOCK] = y
        blk = c // COL_BLOCK
        if blk % 2 == 1:
            cz = (blk // 2) * COL_BLOCK
            z_ref[:, cz:cz + COL_BLOCK] = _dot(hb_ref[...], wz_ref[:, cz:cz + COL_BLOCK])

    def gates():
        ba_t = _dot_nt(wbat_ref[...], hb_ref[...])
        beta_t = jax.nn.sigmoid(ba_t[:V_HEADS])
        loga_t = -jnp.exp(alog_ref[...]) * _softplus(ba_t[V_HEADS:] + dtb_ref[...])
        row = lax.broadcasted_iota(jnp.int32, (chunk, chunk), 0)
        col = lax.broadcasted_iota(jnp.int32, (chunk, chunk), 1)
        upper = (row <= col).astype(F32)
        for j in range(tb // chunk):
            js = slice(j * chunk, (j + 1) * chunk)
            beta_ref[j] = beta_t[:, js]
            g_ref[j] = jnp.dot(loga_t[:, js], upper, precision=HIGHEST, preferred_element_type=F32)

    return [norm] + [functools.partial(conv_block, c) for c in range(0, D_QKV, COL_BLOCK)] + [gates]


def _gdn_proj_kernel(x_ref, cbuf_ref, nw_ref, wqkv_ref, wz_ref, wbat_ref, wconv_ref,
                     alog_ref, dtb_ref,
                     q_ref, k_ref, v_ref, z_ref, beta_ref, g_ref, cnew_ref,
                     hb_ref, *, chunk):
    n_streams = cbuf_ref.shape[0]
    _run(_proj_steps(x_ref, hb_ref, [cbuf_ref.at[s] for s in range(n_streams)], False,
                     nw_ref, wqkv_ref, wz_ref, wbat_ref, wconv_ref, alog_ref, dtb_ref,
                     q_ref, k_ref, v_ref, z_ref, beta_ref, g_ref,
                     [cnew_ref.at[s] for s in range(n_streams)], chunk=chunk))


def _ffn_proj_kernel(x0_ref, cbuf_ref, nwf_ref, wg_ref, wu_ref, wd_ref,
                     nwp_ref, wqkv_ref, wz_ref, wbat_ref, wconv_ref, alog_ref, dtb_ref,
                     x1_ref, q_ref, k_ref, v_ref, z_ref, beta_ref, g_ref, cnew_ref,
                     x1s_ref, hbf_ref, act_ref, hbp_ref, hist_ref, *, chunk, tiles_per_stream):
    i = pl.program_id(0)

    @pl.when(i == 0)
    def _():
        x1s_ref[...] = jnp.zeros_like(x1s_ref)
        hist_ref[...] = jnp.zeros_like(hist_ref)

    @pl.when((i >= 1) & ((i - 1) % tiles_per_stream == 0))
    def _():
        hist_ref[...] = cbuf_ref[0]

    ffn = _ffn_steps(x0_ref, nwf_ref, wg_ref, wu_ref, wd_ref, hbf_ref, act_ref, [x1_ref, x1s_ref])
    proj = _proj_steps(x1s_ref, hbp_ref, [hist_ref], True, nwp_ref, wqkv_ref, wz_ref, wbat_ref, wconv_ref,
                       alog_ref, dtb_ref, q_ref, k_ref, v_ref, z_ref, beta_ref, g_ref, [cnew_ref.at[0]],
                       chunk=chunk)
    proj[0]()
    _run(_interleave(ffn, proj[1:]))


def _gdn_proj(x, conv_buf, norm_w, w_in, w_conv, a_log, dt_bias, *, t_len, chunk, tb):
    n = x.shape[0]
    spt = tb // t_len
    npc = tb // chunk
    cbuf = jnp.pad(conv_buf, ((0, 0), (SUBLANES - (CONV_K - 1), 0), (0, 0)))
    w_ba_t = w_in[:, D_QKV + D_V:].T.astype(BF16)
    consts = (norm_w.reshape(1, D_MODEL), _mxu_weight(w_in[:, :D_QKV]),
              _mxu_weight(w_in[:, D_QKV:D_QKV + D_V]), w_ba_t, w_conv,
              a_log.reshape(V_HEADS, 1), dt_bias.reshape(V_HEADS, 1))
    tok = lambda width: pl.BlockSpec((tb, width), lambda i: (i, 0))
    rows = pl.BlockSpec((npc, V_HEADS, chunk), lambda i: (i, 0, 0))
    conv_state = pl.BlockSpec((spt, SUBLANES, D_QKV), lambda i: (i, 0, 0))
    return pl.pallas_call(
        functools.partial(_gdn_proj_kernel, chunk=chunk),
        grid=(n // tb,),
        in_specs=[tok(D_MODEL), conv_state] + [_const_spec(a.shape) for a in consts],
        out_specs=[tok(D_QK), tok(D_QK), tok(D_V), tok(D_V), rows, rows, conv_state],
        out_shape=[
            jax.ShapeDtypeStruct((n, D_QK), BF16),
            jax.ShapeDtypeStruct((n, D_QK), BF16),
            jax.ShapeDtypeStruct((n, D_V), F32),
            jax.ShapeDtypeStruct((n, D_V), F32),
            jax.ShapeDtypeStruct((n // chunk, V_HEADS, chunk), F32),
            jax.ShapeDtypeStruct((n // chunk, V_HEADS, chunk), F32),
            jax.ShapeDtypeStruct((n // t_len, SUBLANES, D_QKV), F32),
        ],
        scratch_shapes=[pltpu.VMEM((tb, D_MODEL), BF16)],
        compiler_params=_params(1),
        name="gdn_proj",
    )(x, cbuf, *consts)


def _ffn_proj(x0, conv_buf, ffn_norm_w, w_gate, w_up, w_down, norm_w, w_in, w_conv, a_log, dt_bias,
              *, t_len, chunk, tb):
    n = x0.shape[0]
    n_tiles = n // tb
    tiles_per_stream = t_len // tb
    npc = tb // chunk
    cbuf = jnp.pad(conv_buf, ((0, 0), (SUBLANES - (CONV_K - 1), 0), (0, 0)))
    w_ba_t = w_in[:, D_QKV + D_V:].T.astype(BF16)
    consts = (ffn_norm_w.reshape(1, D_MODEL), _mxu_weight(w_gate), _mxu_weight(w_up), _mxu_weight(w_down),
              norm_w.reshape(1, D_MODEL), _mxu_weight(w_in[:, :D_QKV]),
              _mxu_weight(w_in[:, D_QKV:D_QKV + D_V]), w_ba_t, w_conv,
              a_log.reshape(V_HEADS, 1), dt_bias.reshape(V_HEADS, 1))
    ffn_tile = lambda i: jnp.minimum(i, n_tiles - 1)
    proj_tile = lambda i: jnp.maximum(i - 1, 0)
    tok = lambda width: pl.BlockSpec((tb, width), lambda i: (proj_tile(i), 0))
    rows = pl.BlockSpec((npc, V_HEADS, chunk), lambda i: (proj_tile(i), 0, 0))
    conv_state = pl.BlockSpec((1, SUBLANES, D_QKV), lambda i: (proj_tile(i) // tiles_per_stream, 0, 0))
    return pl.pallas_call(
        functools.partial(_ffn_proj_kernel, chunk=chunk, tiles_per_stream=tiles_per_stream),
        grid=(n_tiles + 1,),
        in_specs=[pl.BlockSpec((tb, D_MODEL), lambda i: (ffn_tile(i), 0)), conv_state]
        + [_const_spec(a.shape) for a in consts],
        out_specs=[pl.BlockSpec((tb, D_MODEL), lambda i: (ffn_tile(i), 0)),
                   tok(D_QK), tok(D_QK), tok(D_V), tok(D_V), rows, rows, conv_state],
        out_shape=[
            jax.ShapeDtypeStruct((n, D_MODEL), F32),
            jax.ShapeDtypeStruct((n, D_QK), BF16),
            jax.ShapeDtypeStruct((n, D_QK), BF16),
            jax.ShapeDtypeStruct((n, D_V), F32),
            jax.ShapeDtypeStruct((n, D_V), F32),
            jax.ShapeDtypeStruct((n // chunk, V_HEADS, chunk), F32),
            jax.ShapeDtypeStruct((n // chunk, V_HEADS, chunk), F32),
            jax.ShapeDtypeStruct((n // t_len, SUBLANES, D_QKV), F32),
        ],
        scratch_shapes=[pltpu.VMEM((tb, D_MODEL), F32), pltpu.VMEM((tb, D_MODEL), BF16),
                        pltpu.VMEM((tb, D_FF), BF16), pltpu.VMEM((tb, D_MODEL), BF16),
                        pltpu.VMEM((SUBLANES, D_QKV), F32)],
        compiler_params=_params(1),
        name="ffn_gdn_proj",
    )(x0, cbuf, *consts)


def _gdn_core_kernel(q_ref, k_ref, v_ref, beta_ref, g_ref, s0_ref, won_ref, o_ref, s_ref,
                     t_ref, p_ref, x_ref, attn_ref, kdt_ref, col_ref, ks_ref, dl_ref, *, chunk):
    @pl.when(pl.program_id(1) == 0)
    def _():
        s_ref[...] = s0_ref[...]

    n_chunks = q_ref.shape[1] // chunk
    row = lax.broadcasted_iota(jnp.int32, (chunk, chunk), 0)
    col = lax.broadcasted_iota(jnp.int32, (chunk, chunk), 1)
    incl = row >= col
    strict = row > col
    eye = (row == col).astype(F32)
    diag_blk = (row // INV_BASE) == (col // INV_BASE)
    base_factors = INV_BASE.bit_length() - 1
    slot = lambda c, h: c * V_HEADS + h

    def kq_of(c, pr):
        rs = slice(c * chunk, (c + 1) * chunk)
        hs = slice(pr * HEAD_DIM, (pr + 1) * HEAD_DIM)
        return jnp.concatenate([k_ref[0, rs, hs], q_ref[0, rs, hs]], axis=0)

    for c in range(n_chunks):
        g_rows = g_ref[0, c]
        rows_t = jnp.concatenate([beta_ref[0, c], g_rows, jnp.exp(g_rows)], axis=0)
        col_ref[c] = rows_t.T

    def build_operands(c, pr):
        kq = kq_of(c, pr)
        kq_kt = _dot_nt(kq, kq[:chunk])
        k_t = kq[:chunk].astype(F32).T
        for h in (2 * pr, 2 * pr + 1):
            g_row = g_ref[0, c, h:h + 1, :]
            k_dec = jnp.exp(g_row[:, chunk - 1:chunk] - g_row)
            kdt_ref[slot(c, h)] = (k_t * k_dec).astype(BF16)
            b_col = col_ref[c, :, h:h + 1]
            g_col = col_ref[c, :, V_HEADS + h:V_HEADS + h + 1]
            decay = jnp.where(incl, jnp.exp(jnp.where(incl, g_col - g_row, 0.0)), 0.0)
            x_mat = jnp.where(strict, -(b_col * kq_kt[:chunk] * decay), 0.0)
            x_diag = jnp.where(diag_blk, x_mat, 0.0)
            x_ref[slot(c, h)] = x_mat.astype(BF16)
            t_ref[slot(c, h)] = eye + x_diag
            p_ref[slot(c, h)] = x_diag.astype(BF16)
            attn_ref[slot(c, h)] = jnp.where(incl, kq_kt[chunk:] * decay, 0.0).astype(BF16)

    def square_round(c):
        for h in range(V_HEADS):
            xd = p_ref[slot(c, h)]
            p_ref[slot(c, h)] = _dot(xd, xd).astype(BF16)

    def base_round(c, j):
        for h in range(V_HEADS):
            pb = p_ref[slot(c, h)]
            tm = t_ref[slot(c, h)]
            if j < base_factors - 1:
                pt = _dot(pb, jnp.concatenate([pb, tm.astype(BF16)], axis=1))
                p_ref[slot(c, h)] = pt[:, :chunk].astype(BF16)
                t_ref[slot(c, h)] = tm + pt[:, chunk:]
            else:
                t_ref[slot(c, h)] = tm + _dot(pb, tm.astype(BF16))

    def widen_round(c, width):
        off = ((row // (2 * width)) == (col // (2 * width))) & ((row // width) != (col // width))
        for h in range(V_HEADS):
            tb16 = t_ref[slot(c, h)].astype(BF16)
            w_mat = _dot(jnp.where(off, x_ref[slot(c, h)], jnp.zeros((), BF16)), tb16)
            t_new = t_ref[slot(c, h)] + _dot(tb16, w_mat.astype(BF16))
            if 2 * width < chunk:
                t_ref[slot(c, h)] = t_new
            else:
                p_ref[slot(c, h)] = t_new.astype(BF16)

    def inverse_rounds(c):
        rounds = [functools.partial(square_round, c)]
        rounds += [functools.partial(base_round, c, j) for j in range(1, base_factors)]
        width = INV_BASE
        while width < chunk:
            rounds.append(functools.partial(widen_round, c, width))
            width *= 2
        return rounds

    for c in range(n_chunks):
        for pr in range(K_HEADS):
            build_operands(c, pr)
    for same_round in zip(*[inverse_rounds(c) for c in range(n_chunks)]):
        _run(same_round)

    for c in range(n_chunks):
        rs = slice(c * chunk, (c + 1) * chunk)
        for pr in range(K_HEADS):
            s_pair = jnp.concatenate([s_ref[0, 2 * pr], s_ref[0, 2 * pr + 1]], axis=1).astype(BF16)
            ks_ref[pr] = _dot(kq_of(c, pr), s_pair)
        for h in range(V_HEADS):
            es = slice((h % 2) * HEAD_DIM, (h % 2 + 1) * HEAD_DIM)
            b_col = col_ref[c, :, h:h + 1]
            gam = col_ref[c, :, 2 * V_HEADS + h:2 * V_HEADS + h + 1]
            rhs = b_col * (v_ref[0, rs, h * HEAD_DIM:(h + 1) * HEAD_DIM] - gam * ks_ref[h // 2, :chunk, es])
            dl_ref[h] = _dot(p_ref[slot(c, h)], rhs.astype(BF16)).astype(BF16)
        for h in range(V_HEADS):
            s_decay = jnp.exp(g_ref[0, c, h:h + 1, chunk - 1:chunk])
            s_ref[0, h] = s_decay * s_ref[0, h] + _dot(kdt_ref[slot(c, h)], dl_ref[h])
        for h in range(V_HEADS):
            es = slice((h % 2) * HEAD_DIM, (h % 2 + 1) * HEAD_DIM)
            gam = col_ref[c, :, 2 * V_HEADS + h:2 * V_HEADS + h + 1]
            o = gam * ks_ref[h // 2, chunk:, es] + _dot(attn_ref[slot(c, h)], dl_ref[h])
            o_ref[0, rs, h * HEAD_DIM:(h + 1) * HEAD_DIM] = o * _rms_scale(o) * won_ref[...]


def _gdn_core(q, k, v, beta_t, g_t, s0, w_onorm, *, chunk, tb):
    b, t_len, _ = q.shape
    nt = t_len // tb
    npc = tb // chunk
    tok = lambda width: pl.BlockSpec((1, tb, width), lambda i, j: (i, j, 0))
    rows = pl.BlockSpec((1, npc, V_HEADS, chunk), lambda i, j: (i, j, 0, 0))
    state = pl.BlockSpec((1, V_HEADS, HEAD_DIM, HEAD_DIM), lambda i, j: (i, 0, 0, 0))
    return pl.pallas_call(
        functools.partial(_gdn_core_kernel, chunk=chunk),
        grid=(b, nt),
        in_specs=[tok(D_QK), tok(D_QK), tok(D_V), rows, rows, state, _const_spec((1, HEAD_DIM))],
        out_specs=[tok(D_V), state],
        out_shape=[jax.ShapeDtypeStruct((b, t_len, D_V), F32),
                   jax.ShapeDtypeStruct((b, V_HEADS, HEAD_DIM, HEAD_DIM), F32)],
        scratch_shapes=[pltpu.VMEM((npc * V_HEADS, chunk, chunk), F32),
                        pltpu.VMEM((npc * V_HEADS, chunk, chunk), BF16),
                        pltpu.VMEM((npc * V_HEADS, chunk, chunk), BF16),
                        pltpu.VMEM((npc * V_HEADS, chunk, chunk), BF16),
                        pltpu.VMEM((npc * V_HEADS, HEAD_DIM, chunk), BF16),
                        pltpu.VMEM((npc, chunk, 3 * V_HEADS), F32),
                        pltpu.VMEM((K_HEADS, 2 * chunk, 2 * HEAD_DIM), F32),
                        pltpu.VMEM((V_HEADS, chunk, HEAD_DIM), BF16)],
        compiler_params=_params(2),
        name="gdn_core",
    )(q, k, v, beta_t, g_t, s0, w_onorm.reshape(1, HEAD_DIM))


def _gdn_tail_kernel(o_ref, z_ref, x_ref, wout_ref, nw_ref, wg_ref, wu_ref, wd_ref, nf_ref,
                     y_ref, gated_ref, mid_ref, hb_ref, act_ref):
    for c in range(0, D_V, COL_BLOCK):
        cs = slice(c, c + COL_BLOCK)
        gated_ref[:, cs] = (o_ref[:, cs] * _silu(z_ref[:, cs])).astype(BF16)
    _residual_dot_store(x_ref, gated_ref, wout_ref, mid_ref)
    _run(_ffn_steps(mid_ref, nw_ref, wg_ref, wu_ref, wd_ref, hb_ref, act_ref, [y_ref], nf_ref))


def _gdn_tail(o, z, x, w_out, norm_w, w_gate, w_up, w_down, norm_final, *, tb):
    n = x.shape[0]
    consts = (_mxu_weight(w_out), norm_w.reshape(1, D_MODEL), _mxu_weight(w_gate), _mxu_weight(w_up),
              _mxu_weight(w_down), norm_final.reshape(1, D_MODEL))
    return pl.pallas_call(
        _gdn_tail_kernel,
        grid=(n // tb,),
        in_specs=[pl.BlockSpec((tb, D_V), lambda i: (i, 0)), pl.BlockSpec((tb, D_V), lambda i: (i, 0)),
                  pl.BlockSpec((tb, D_MODEL), lambda i: (i, 0))] + [_const_spec(a.shape) for a in consts],
        out_specs=pl.BlockSpec((tb, D_MODEL), lambda i: (i, 0)),
        out_shape=jax.ShapeDtypeStruct((n, D_MODEL), F32),
        scratch_shapes=[pltpu.VMEM((tb, D_V), BF16), pltpu.VMEM((tb, D_MODEL), F32),
                        pltpu.VMEM((tb, D_MODEL), BF16), pltpu.VMEM((tb, D_FF), BF16)],
        compiler_params=_params(1),
        name="gdn_out_ffn",
    )(o, z, x, *consts)


def _trunk(x, conv_buf, s0, p, *, sgu_chunk, gdn_chunk, tb, gdn_tb, core_tb, write_v):
    b, t_len, _ = x.shape
    flat = lambda a: a.reshape(b * t_len, a.shape[-1])
    x0, v_rows = _sgu_layer(flat(x), p["norm_mix"][0], p["sgu_w_in"][0], p["sgu_ln_g"][0],
                            p["sgu_ln_b"][0], p["sgu_w_s"][0], p["sgu_b_s"][0], p["sgu_w_out"][0],
                            chunk=sgu_chunk, tb=tb, write_v=write_v)
    ffn_w = (p["norm_ffn"][0], p["ffn_w_gate"][0], p["ffn_w_up"][0], p["ffn_w_down"][0])
    proj_w = (p["norm_mix"][1], p["gdn_w_in"][0], p["gdn_w_conv"][0], p["gdn_a_log"][0], p["gdn_dt_bias"][0])
    if t_len > gdn_tb:
        x1, *proj = _ffn_proj(x0, conv_buf, *ffn_w, *proj_w, t_len=t_len, chunk=gdn_chunk, tb=gdn_tb)
    else:
        x1 = _ffn_layer(x0, *ffn_w, tb=tb)
        proj = _gdn_proj(x1, conv_buf, *proj_w, t_len=t_len, chunk=gdn_chunk, tb=gdn_tb)
    q, k, v, z, beta_t, g_t = (a.reshape(b, a.shape[0] // b, *a.shape[1:]) for a in proj[:-1])
    conv_tail = proj[-1]
    o, s_new = _gdn_core(q, k, v, beta_t, g_t, s0, p["gdn_w_onorm"][0], chunk=gdn_chunk, tb=core_tb)
    y = _gdn_tail(flat(o), flat(z), x1, p["gdn_w_out"][0], p["norm_ffn"][1], p["ffn_w_gate"][1],
                  p["ffn_w_up"][1], p["ffn_w_down"][1], p["norm_final"], tb=tb)
    conv_new = conv_tail[:, SUBLANES - (CONV_K - 1):, :]
    if write_v:
        v_rows = v_rows.reshape(b, t_len, D_SGU)
    return y.reshape(b, t_len, D_MODEL), s_new, conv_new, v_rows


def kernel(x_prompt, x_sample, state_gdn, state_conv, norm_mix, norm_ffn, norm_final, sgu_w_in, sgu_ln_g, sgu_ln_b, sgu_w_s, sgu_b_s, sgu_w_out, gdn_w_in, gdn_w_conv, gdn_a_log, gdn_dt_bias, gdn_w_onorm, gdn_w_out, ffn_w_gate, ffn_w_up, ffn_w_down):
    p = dict(norm_mix=norm_mix, norm_ffn=norm_ffn, norm_final=norm_final, sgu_w_in=sgu_w_in,
             sgu_ln_g=sgu_ln_g, sgu_ln_b=sgu_ln_b, sgu_w_s=sgu_w_s, sgu_b_s=sgu_b_s,
             sgu_w_out=sgu_w_out, gdn_w_in=gdn_w_in, gdn_w_conv=gdn_w_conv, gdn_a_log=gdn_a_log,
             gdn_dt_bias=gdn_dt_bias, gdn_w_onorm=gdn_w_onorm, gdn_w_out=gdn_w_out,
             ffn_w_gate=ffn_w_gate, ffn_w_up=ffn_w_up, ffn_w_down=ffn_w_down)
    bp = x_prompt.shape[0]
    dec_seq = x_sample.shape[1]
    zero_conv = jnp.zeros((bp, CONV_K - 1, D_QKV), F32)
    zero_state = jnp.zeros((bp, V_HEADS, HEAD_DIM, HEAD_DIM), F32)
    yp, sp, cp, _ = _trunk(x_prompt, zero_conv, zero_state, p, sgu_chunk=SGU_CHUNK,
                           gdn_chunk=2 * GDN_CHUNK, tb=256, gdn_tb=256, core_tb=256, write_v=False)
    ys, ss, cs, vs = _trunk(x_sample, state_conv[0], state_gdn[0], p, sgu_chunk=dec_seq,
                            gdn_chunk=dec_seq, tb=256, gdn_tb=256, core_tb=dec_seq, write_v=True)
    return (yp, ys, sp[None], cp[None], ss[None], cs[None], vs[None])
```

```python
import functools
import math

import jax
import jax.numpy as jnp
from jax import lax
from jax.experimental import pallas as pl
from jax.experimental.pallas import tpu as pltpu

D_MODEL = 1024
D_SGU = 2048
SGU_GROUPS = 8
SGU_GROUP_DIM = D_SGU // SGU_GROUPS
SGU_CHUNK = 128
HEAD_DIM = 128
K_HEADS = 8
V_HEADS = 16
D_QK = K_HEADS * HEAD_DIM
D_V = V_HEADS * HEAD_DIM
D_QKV = 2 * D_QK + D_V
CONV_K = 4
GDN_CHUNK = 64
D_FF = 2816
EPS = 1e-6
LN_EPS = 1e-5

VMEM_LIMIT_BYTES = 56 * 1024 * 1024
SUBLANES = 8
COL_BLOCK = 256
LANE_PAIR_TILE = 512
INV_BASE = 8

F32 = jnp.float32
BF16 = jnp.bfloat16
HIGHEST = lax.Precision.HIGHEST


def _const_spec(shape):
    nd = len(shape)
    return pl.BlockSpec(shape, lambda *_: (0,) * nd, pipeline_mode=pl.Buffered(1))


def _mxu_weight(w):
    wb = w.astype(BF16)
    if w.shape[-1] % LANE_PAIR_TILE == 0:
        wb = jnp.pad(wb, ((0, 0), (0, COL_BLOCK)))
    return wb


def _params(n_grid):
    return pltpu.CompilerParams(dimension_semantics=("arbitrary",) * n_grid,
                                vmem_limit_bytes=VMEM_LIMIT_BYTES)


def _rms_scale(x):
    return lax.rsqrt(jnp.mean(x * x, axis=-1, keepdims=True) + EPS)


def _silu(x):
    return x * jax.nn.sigmoid(x)


def _gelu_tanh(x):
    return 0.5 * x * (1.0 + jnp.tanh(math.sqrt(2.0 / math.pi) * (x + 0.044715 * (x * x * x))))


def _softplus(x):
    return jnp.maximum(x, 0.0) + jnp.log1p(jnp.exp(-jnp.abs(x)))


def _dot(a, b):
    return jnp.dot(a, b, preferred_element_type=F32)


def _dot_nt(a, b, precision=None):
    return lax.dot_general(a, b, (((1,), (1,)), ((), ())), precision=precision,
                           preferred_element_type=F32)


def _dot_tn(a, b):
    return lax.dot_general(a, b, (((0,), (0,)), ((), ())), preferred_element_type=F32)


def _residual_dot_store(x_ref, a_ref, w_ref, o_ref):
    ssq = jnp.zeros((x_ref.shape[0], 1), F32)
    for c in range(0, x_ref.shape[1], COL_BLOCK):
        cs = slice(c, c + COL_BLOCK)
        y = x_ref[:, cs] + _dot(a_ref[...], w_ref[:, cs])
        o_ref[:, cs] = y
        ssq = ssq + jnp.sum(y * y, axis=-1, keepdims=True)
    return ssq


def _run(steps):
    for step in steps:
        step()


def _interleave(a_steps, b_steps):
    keyed = [((i + 0.5) / len(a_steps), 0, s) for i, s in enumerate(a_steps)]
    keyed += [((j + 0.5) / len(b_steps), 1, s) for j, s in enumerate(b_steps)]
    return [s for _, _, s in sorted(keyed, key=lambda t: t[:2])]


def _ffn_steps(x_ref, nw_ref, wg_ref, wu_ref, wd_ref, hb_ref, act_ref, o_refs, nf_ref=None):
    def norm():
        x = x_ref[...]
        hb_ref[...] = (x * _rms_scale(x) * nw_ref[...]).astype(BF16)

    def gate_up(c):
        g = _dot(hb_ref[...], wg_ref[:, c:c + COL_BLOCK])
        u = _dot(hb_ref[...], wu_ref[:, c:c + COL_BLOCK])
        act_ref[:, c:c + COL_BLOCK] = (_silu(g) * u).astype(BF16)

    ssq = [jnp.zeros((x_ref.shape[0], 1), F32)]

    def down(c):
        cs = slice(c, c + COL_BLOCK)
        y = x_ref[:, cs] + _dot(act_ref[...], wd_ref[:, cs])
        for o_ref in o_refs:
            o_ref[:, cs] = y
        if nf_ref is not None:
            ssq[0] = ssq[0] + jnp.sum(y * y, axis=-1, keepdims=True)

    def final_norm():
        for o_ref in o_refs:
            o_ref[...] = o_ref[...] * lax.rsqrt(ssq[0] * (1.0 / D_MODEL) + EPS) * nf_ref[...]

    steps = [norm] + [functools.partial(gate_up, c) for c in range(0, D_FF, COL_BLOCK)]
    steps += [functools.partial(down, c) for c in range(0, D_MODEL, COL_BLOCK)]
    return steps + ([final_norm] if nf_ref is not None else [])


def _sgu_kernel(x_ref, nw_ref, win_ref, lng_ref, lnb_ref, ws_ref, bs_ref, wout_ref,
                *rest, chunk, write_v):
    if write_v:
        o_ref, v_ref, hb_ref, u_ref, vv_ref, gated_ref = rest
    else:
        o_ref, hb_ref, u_ref, vv_ref, gated_ref = rest
        v_ref = None
    tb = x_ref.shape[0]
    x = x_ref[...]
    hb_ref[...] = (x * _rms_scale(x) * nw_ref[...]).astype(BF16)

    vsum = jnp.zeros((tb, 1), F32)
    for c in range(0, 2 * D_SGU, COL_BLOCK):
        uv = _gelu_tanh(_dot(hb_ref[...], win_ref[:, c:c + COL_BLOCK]))
        if c < D_SGU:
            u_ref[:, c:c + COL_BLOCK] = uv
        else:
            vv_ref[:, c - D_SGU:c - D_SGU + COL_BLOCK] = uv
            vsum = vsum + jnp.sum(uv, axis=-1, keepdims=True)
    mu = vsum * (1.0 / D_SGU)
    vvar = jnp.zeros((tb, 1), F32)
    for c in range(0, D_SGU, COL_BLOCK):
        d = vv_ref[:, c:c + COL_BLOCK] - mu
        vvar = vvar + jnp.sum(d * d, axis=-1, keepdims=True)
    rstd = lax.rsqrt(vvar * (1.0 / D_SGU) + LN_EPS)

    row = lax.broadcasted_iota(jnp.int32, (chunk, chunk), 0)
    col = lax.broadcasted_iota(jnp.int32, (chunk, chunk), 1)
    causal = row >= col
    for g in range(SGU_GROUPS):
        c0 = g * SGU_GROUP_DIM
        cs = slice(c0, c0 + SGU_GROUP_DIM)
        vn = (vv_ref[:, cs] - mu) * rstd * lng_ref[:, cs] + lnb_ref[:, cs]
        if write_v:
            v_ref[:, cs] = vn
        vnb = vn.astype(BF16)
        wsg = jnp.where(causal, ws_ref[g], 0.0).astype(BF16)
        bias = bs_ref[:, g:g + 1]
        for r in range(0, tb, chunk):
            mixed = _dot(wsg, vnb[r:r + chunk]) + bias
            gated_ref[r:r + chunk, cs] = (u_ref[r:r + chunk, cs] * mixed).astype(BF16)
    _residual_dot_store(x_ref, gated_ref, wout_ref, o_ref)


def _sgu_layer(x, norm_w, w_in, ln_g, ln_b, w_s, b_s, w_out, *, chunk, tb, write_v):
    n = x.shape[0]
    ws = w_s[:, :chunk, :chunk]
    bs_t = b_s[:, :chunk].T
    out_shape = [jax.ShapeDtypeStruct((n, D_MODEL), F32)]
    out_specs = [pl.BlockSpec((tb, D_MODEL), lambda i: (i, 0))]
    if write_v:
        out_shape.append(jax.ShapeDtypeStruct((n, D_SGU), F32))
        out_specs.append(pl.BlockSpec((tb, D_SGU), lambda i: (i, 0)))
    consts = (norm_w.reshape(1, D_MODEL), _mxu_weight(w_in), ln_g.reshape(1, D_SGU),
              ln_b.reshape(1, D_SGU), ws, bs_t, _mxu_weight(w_out))
    res = pl.pallas_call(
        functools.partial(_sgu_kernel, chunk=chunk, write_v=write_v),
        grid=(n // tb,),
        in_specs=[pl.BlockSpec((tb, D_MODEL), lambda i: (i, 0))] + [_const_spec(a.shape) for a in consts],
        out_specs=out_specs,
        out_shape=out_shape,
        scratch_shapes=[
            pltpu.VMEM((tb, D_MODEL), BF16),
            pltpu.VMEM((tb, D_SGU), F32),
            pltpu.VMEM((tb, D_SGU), F32),
            pltpu.VMEM((tb, D_SGU), BF16),
        ],
        compiler_params=_params(1),
        name="sgu_mixer",
    )(x, *consts)
    return res if write_v else (res[0], None)


def _ffn_kernel(x_ref, nw_ref, wg_ref, wu_ref, wd_ref, o_ref, hb_ref, act_ref):
    _run(_ffn_steps(x_ref, nw_ref, wg_ref, wu_ref, wd_ref, hb_ref, act_ref, [o_ref]))


def _ffn_layer(x, norm_w, w_gate, w_up, w_down, *, tb):
    n = x.shape[0]
    consts = (norm_w.reshape(1, D_MODEL), _mxu_weight(w_gate), _mxu_weight(w_up), _mxu_weight(w_down))
    return pl.pallas_call(
        _ffn_kernel,
        grid=(n // tb,),
        in_specs=[pl.BlockSpec((tb, D_MODEL), lambda i: (i, 0))] + [_const_spec(a.shape) for a in consts],
        out_specs=pl.BlockSpec((tb, D_MODEL), lambda i: (i, 0)),
        out_shape=jax.ShapeDtypeStruct((n, D_MODEL), F32),
        scratch_shapes=[pltpu.VMEM((tb, D_MODEL), BF16), pltpu.VMEM((tb, D_FF), BF16)],
        compiler_params=_params(1),
        name="ffn",
    )(x, *consts)


def _proj_steps(x_ref, hb_ref, hist_refs, carry, nw_ref, wqkv_ref, wz_ref, wbat_ref, wconv_ref, alog_ref,
                dtb_ref, q_ref, k_ref, v_ref, z_ref, beta_ref, g_ref, cnew_refs, *, chunk):
    tb = x_ref.shape[0]
    seg = tb // len(hist_refs)

    def norm():
        x = x_ref[...]
        hb_ref[...] = (x * _rms_scale(x) * nw_ref[...]).astype(BF16)

    def conv_block(c):
        cs = slice(c, c + COL_BLOCK)
        pre_all = _dot(hb_ref[...], wqkv_ref[:, cs])
        for s, (hist_ref, cnew_ref) in enumerate(zip(hist_refs, cnew_refs)):
            rs = slice(s * seg, (s + 1) * seg)
            pre = pre_all[rs]
            padded = jnp.concatenate([hist_ref[:, cs], pre], axis=0)
            if carry:
                hist_ref[:, cs] = pre[seg - SUBLANES:, :]
            cnew_ref[:, cs] = pre[seg - SUBLANES:, :]
            acc = padded * wconv_ref[0:1, cs]
            for i in range(1, CONV_K):
                acc = pltpu.roll(acc, 1, axis=0) + padded * wconv_ref[i:i + 1, cs]
            y = _silu(acc[SUBLANES:, :])
            if c < 2 * D_QK:
                dst, c_dst, scale = (q_ref, c, HEAD_DIM ** -0.5) if c < D_QK else (k_ref, c - D_QK, 1.0)
                for h in range(0, COL_BLOCK, HEAD_DIM):
                    yh = y[:, h:h + HEAD_DIM]
                    inv = lax.rsqrt(jnp.sum(yh * yh, axis=-1, keepdims=True) + EPS) * scale
                    dst[rs, c_dst + h:c_dst + h + HEAD_DIM] = (yh * inv).astype(BF16)
            else:
                v_ref[rs, c - 2 * D_QK:c - 2 * D_QK + COL_BLOCK] = y
        blk = c // COL_BLOCK
        if blk % 2 == 1:
            cz = (blk // 2) * COL_BLOCK
            z_ref[:, cz:cz + COL_BLOCK] = _dot(hb_ref[...], wz_ref[:, cz:cz + COL_BLOCK])

    def gates():
        ba_t = _dot_nt(wbat_ref[...], hb_ref[...])
        beta_t = jax.nn.sigmoid(ba_t[:V_HEADS])
        loga_t = -jnp.exp(alog_ref[...]) * _softplus(ba_t[V_HEADS:] + dtb_ref[...])
        row = lax.broadcasted_iota(jnp.int32, (chunk, chunk), 0)
        col = lax.broadcasted_iota(jnp.int32, (chunk, chunk), 1)
        upper = (row <= col).astype(F32)
        for j in range(tb // chunk):
            js = slice(j * chunk, (j + 1) * chunk)
            beta_ref[j] = beta_t[:, js]
            g_ref[j] = jnp.dot(loga_t[:, js], upper, precision=HIGHEST, preferred_element_type=F32)

    return [norm] + [functools.partial(conv_block, c) for c in range(0, D_QKV, COL_BLOCK)] + [gates]


def _gdn_proj_kernel(x_ref, cbuf_ref, nw_ref, wqkv_ref, wz_ref, wbat_ref, wconv_ref,
                     alog_ref, dtb_ref,
                     q_ref, k_ref, v_ref, z_ref, beta_ref, g_ref, cnew_ref,
                     hb_ref, *, chunk):
    n_streams = cbuf_ref.shape[0]
    _run(_proj_steps(x_ref, hb_ref, [cbuf_ref.at[s] for s in range(n_streams)], False,
                     nw_ref, wqkv_ref, wz_ref, wbat_ref, wconv_ref, alog_ref, dtb_ref,
                     q_ref, k_ref, v_ref, z_ref, beta_ref, g_ref,
                     [cnew_ref.at[s] for s in range(n_streams)], chunk=chunk))


def _ffn_proj_kernel(x0_ref, cbuf_ref, nwf_ref, wg_ref, wu_ref, wd_ref,
                     nwp_ref, wqkv_ref, wz_ref, wbat_ref, wconv_ref, alog_ref, dtb_ref,
                     x1_ref, q_ref, k_ref, v_ref, z_ref, beta_ref, g_ref, cnew_ref,
                     x1s_ref, hbf_ref, act_ref, hbp_ref, hist_ref, *, chunk, tiles_per_stream):
    i = pl.program_id(0)

    @pl.when(i == 0)
    def _():
        x1s_ref[...] = jnp.zeros_like(x1s_ref)
        hist_ref[...] = jnp.zeros_like(hist_ref)

    @pl.when((i >= 1) & ((i - 1) % tiles_per_stream == 0))
    def _():
        hist_ref[...] = cbuf_ref[0]

    ffn = _ffn_steps(x0_ref, nwf_ref, wg_ref, wu_ref, wd_ref, hbf_ref, act_ref, [x1_ref, x1s_ref])
    proj = _proj_steps(x1s_ref, hbp_ref, [hist_ref], True, nwp_ref, wqkv_ref, wz_ref, wbat_ref, wconv_ref,
                       alog_ref, dtb_ref, q_ref, k_ref, v_ref, z_ref, beta_ref, g_ref, [cnew_ref.at[0]],
                       chunk=chunk)
    proj[0]()
    _run(_interleave(ffn, proj[1:]))


def _gdn_proj(x, conv_buf, norm_w, w_in, w_conv, a_log, dt_bias, *, t_len, chunk, tb):
    n = x.shape[0]
    spt = tb // t_len
    npc = tb // chunk
    cbuf = jnp.pad(conv_buf, ((0, 0), (SUBLANES - (CONV_K - 1), 0), (0, 0)))
    w_ba_t = w_in[:, D_QKV + D_V:].T.astype(BF16)
    consts = (norm_w.reshape(1, D_MODEL), _mxu_weight(w_in[:, :D_QKV]),
              _mxu_weight(w_in[:, D_QKV:D_QKV + D_V]), w_ba_t, w_conv,
              a_log.reshape(V_HEADS, 1), dt_bias.reshape(V_HEADS, 1))
    tok = lambda width: pl.BlockSpec((tb, width), lambda i: (i, 0))
    rows = pl.BlockSpec((npc, V_HEADS, chunk), lambda i: (i, 0, 0))
    conv_state = pl.BlockSpec((spt, SUBLANES, D_QKV), lambda i: (i, 0, 0))
    return pl.pallas_call(
        functools.partial(_gdn_proj_kernel, chunk=chunk),
        grid=(n // tb,),
        in_specs=[tok(D_MODEL), conv_state] + [_const_spec(a.shape) for a in consts],
        out_specs=[tok(D_QK), tok(D_QK), tok(D_V), tok(D_V), rows, rows, conv_state],
        out_shape=[
            jax.ShapeDtypeStruct((n, D_QK), BF16),
            jax.ShapeDtypeStruct((n, D_QK), BF16),
            jax.ShapeDtypeStruct((n, D_V), F32),
            jax.ShapeDtypeStruct((n, D_V), F32),
            jax.ShapeDtypeStruct((n // chunk, V_HEADS, chunk), F32),
            jax.ShapeDtypeStruct((n // chunk, V_HEADS, chunk), F32),
            jax.ShapeDtypeStruct((n // t_len, SUBLANES, D_QKV), F32),
        ],
        scratch_shapes=[pltpu.VMEM((tb, D_MODEL), BF16)],
        compiler_params=_params(1),
        name="gdn_proj",
    )(x, cbuf, *consts)


def _ffn_proj(x0, conv_buf, ffn_norm_w, w_gate, w_up, w_down, norm_w, w_in, w_conv, a_log, dt_bias,
              *, t_len, chunk, tb):
    n = x0.shape[0]
    n_tiles = n // tb
    tiles_per_stream = t_len // tb
    npc = tb // chunk
    cbuf = jnp.pad(conv_buf, ((0, 0), (SUBLANES - (CONV_K - 1), 0), (0, 0)))
    w_ba_t = w_in[:, D_QKV + D_V:].T.astype(BF16)
    consts = (ffn_norm_w.reshape(1, D_MODEL), _mxu_weight(w_gate), _mxu_weight(w_up), _mxu_weight(w_down),
              norm_w.reshape(1, D_MODEL), _mxu_weight(w_in[:, :D_QKV]),
              _mxu_weight(w_in[:, D_QKV:D_QKV + D_V]), w_ba_t, w_conv,
              a_log.reshape(V_HEADS, 1), dt_bias.reshape(V_HEADS, 1))
    ffn_tile = lambda i: jnp.minimum(i, n_tiles - 1)
    proj_tile = lambda i: jnp.maximum(i - 1, 0)
    tok = lambda width: pl.BlockSpec((tb, width), lambda i: (proj_tile(i), 0))
    rows = pl.BlockSpec((npc, V_HEADS, chunk), lambda i: (proj_tile(i), 0, 0))
    conv_state = pl.BlockSpec((1, SUBLANES, D_QKV), lambda i: (proj_tile(i) // tiles_per_stream, 0, 0))
    return pl.pallas_call(
        functools.partial(_ffn_proj_kernel, chunk=chunk, tiles_per_stream=tiles_per_stream),
        grid=(n_tiles + 1,),
        in_specs=[pl.BlockSpec((tb, D_MODEL), lambda i: (ffn_tile(i), 0)), conv_state]
        + [_const_spec(a.shape) for a in consts],
        out_specs=[pl.BlockSpec((tb, D_MODEL), lambda i: (ffn_tile(i), 0)),
                   tok(D_QK), tok(D_QK), tok(D_V), tok(D_V), rows, rows, conv_state],
        out_shape=[
            jax.ShapeDtypeStruct((n, D_MODEL), F32),
            jax.ShapeDtypeStruct((n, D_QK), BF16),
            jax.ShapeDtypeStruct((n, D_QK), BF16),
            jax.ShapeDtypeStruct((n, D_V), F32),
            jax.ShapeDtypeStruct((n, D_V), F32),
            jax.ShapeDtypeStruct((n // chunk, V_HEADS, chunk), F32),
            jax.ShapeDtypeStruct((n // chunk, V_HEADS, chunk), F32),
            jax.ShapeDtypeStruct((n // t_len, SUBLANES, D_QKV), F32),
        ],
        scratch_shapes=[pltpu.VMEM((tb, D_MODEL), F32), pltpu.VMEM((tb, D_MODEL), BF16),
                        pltpu.VMEM((tb, D_FF), BF16), pltpu.VMEM((tb, D_MODEL), BF16),
                        pltpu.VMEM((SUBLANES, D_QKV), F32)],
        compiler_params=_params(1),
        name="ffn_gdn_proj",
    )(x0, cbuf, *consts)


def _gdn_core_kernel(q_ref, k_ref, v_ref, beta_ref, g_ref, s0_ref, won_ref, o_ref, s_ref,
                     t_ref, p_ref, x_ref, attn_ref, kdt_ref, col_ref, ks_ref, dl_ref, *, chunk):
    @pl.when(pl.program_id(1) == 0)
    def _():
        s_ref[...] = s0_ref[...]

    n_chunks = q_ref.shape[1] // chunk
    row = lax.broadcasted_iota(jnp.int32, (chunk, chunk), 0)
    col = lax.broadcasted_iota(jnp.int32, (chunk, chunk), 1)
    incl = row >= col
    strict = row > col
    eye = (row == col).astype(F32)
    diag_blk = (row // INV_BASE) == (col // INV_BASE)
    base_factors = INV_BASE.bit_length() - 1
    slot = lambda c, h: c * V_HEADS + h

    def kq_of(c, pr):
        rs = slice(c * chunk, (c + 1) * chunk)
        hs = slice(pr * HEAD_DIM, (pr + 1) * HEAD_DIM)
        return jnp.concatenate([k_ref[0, rs, hs], q_ref[0, rs, hs]], axis=0)

    for c in range(n_chunks):
        g_rows = g_ref[0, c]
        rows_t = jnp.concatenate([beta_ref[0, c], g_rows, jnp.exp(g_rows)], axis=0)
        col_ref[c] = rows_t.T

    def build_operands(c, pr):
        kq = kq_of(c, pr)
        kq_kt = _dot_nt(kq, kq[:chunk])
        k_t = kq[:chunk].astype(F32).T
        for h in (2 * pr, 2 * pr + 1):
            g_row = g_ref[0, c, h:h + 1, :]
            k_dec = jnp.exp(g_row[:, chunk - 1:chunk] - g_row)
            kdt_ref[slot(c, h)] = (k_t * k_dec).astype(BF16)
            b_col = col_ref[c, :, h:h + 1]
            g_col = col_ref[c, :, V_HEADS + h:V_HEADS + h + 1]
            decay = jnp.where(incl, jnp.exp(jnp.where(incl, g_col - g_row, 0.0)), 0.0)
            x_mat = jnp.where(strict, -(b_col * kq_kt[:chunk] * decay), 0.0)
            x_diag = jnp.where(diag_blk, x_mat, 0.0)
            x_ref[slot(c, h)] = x_mat.astype(BF16)
            t_ref[slot(c, h)] = eye + x_diag
            p_ref[slot(c, h)] = x_diag.astype(BF16)
            attn_ref[slot(c, h)] = jnp.where(incl, kq_kt[chunk:] * decay, 0.0).astype(BF16)

    def square_round(c):
        for h in range(V_HEADS):
            xd = p_ref[slot(c, h)]
            p_ref[slot(c, h)] = _dot(xd, xd).astype(BF16)

    def base_round(c, j):
        for h in range(V_HEADS):
            pb = p_ref[slot(c, h)]
            tm = t_ref[slot(c, h)]
            if j < base_factors - 1:
                pt = _dot(pb, jnp.concatenate([pb, tm.astype(BF16)], axis=1))
                p_ref[slot(c, h)] = pt[:, :chunk].astype(BF16)
                t_ref[slot(c, h)] = tm + pt[:, chunk:]
            else:
                t_ref[slot(c, h)] = tm + _dot(pb, tm.astype(BF16))

    def widen_round(c, width):
        off = ((row // (2 * width)) == (col // (2 * width))) & ((row // width) != (col // width))
        for h in range(V_HEADS):
            tb16 = t_ref[slot(c, h)].astype(BF16)
            w_mat = _dot(jnp.where(off, x_ref[slot(c, h)], jnp.zeros((), BF16)), tb16)
            t_new = t_ref[slot(c, h)] + _dot(tb16, w_mat.astype(BF16))
            if 2 * width < chunk:
                t_ref[slot(c, h)] = t_new
            else:
                p_ref[slot(c, h)] = t_new.astype(BF16)

    def inverse_rounds(c):
        rounds = [functools.partial(square_round, c)]
        rounds += [functools.partial(base_round, c, j) for j in range(1, base_factors)]
        width = INV_BASE
        while width < chunk:
            rounds.append(functools.partial(widen_round, c, width))
            width *= 2
        return rounds

    for c in range(n_chunks):
        for pr in range(K_HEADS):
            build_operands(c, pr)
    for same_round in zip(*[inverse_rounds(c) for c in range(n_chunks)]):
        _run(same_round)

    for c in range(n_chunks):
        rs = slice(c * chunk, (c + 1) * chunk)
        for pr in range(K_HEADS):
            s_pair = jnp.concatenate([s_ref[0, 2 * pr], s_ref[0, 2 * pr + 1]], axis=1).astype(BF16)
            ks_ref[pr] = _dot(kq_of(c, pr), s_pair)
        for h in range(V_HEADS):
            es = slice((h % 2) * HEAD_DIM, (h % 2 + 1) * HEAD_DIM)
            b_col = col_ref[c, :, h:h + 1]
            gam = col_ref[c, :, 2 * V_HEADS + h:2 * V_HEADS + h + 1]
            rhs = b_col * (v_ref[0, rs, h * HEAD_DIM:(h + 1) * HEAD_DIM] - gam * ks_ref[h // 2, :chunk, es])
            dl_ref[h] = _dot(p_ref[slot(c, h)], rhs.astype(BF16)).astype(BF16)
        for h in range(V_HEADS):
            s_decay = jnp.exp(g_ref[0, c, h:h + 1, chunk - 1:chunk])
            s_ref[0, h] = s_decay * s_ref[0, h] + _dot(kdt_ref[slot(c, h)], dl_ref[h])
        for h in range(V_HEADS):
            es = slice((h % 2) * HEAD_DIM, (h % 2 + 1) * HEAD_DIM)
            gam = col_ref[c, :, 2 * V_HEADS + h:2 * V_HEADS + h + 1]
            o = gam * ks_ref[h // 2, chunk:, es] + _dot(attn_ref[slot(c, h)], dl_ref[h])
            o_ref[0, rs, h * HEAD_DIM:(h + 1) * HEAD_DIM] = o * _rms_scale(o) * won_ref[...]


def _gdn_core(q, k, v, beta_t, g_t, s0, w_onorm, *, chunk, tb):
    b, t_len, _ = q.shape
    nt = t_len // tb
    npc = tb // chunk
    tok = lambda width: pl.BlockSpec((1, tb, width), lambda i, j: (i, j, 0))
    rows = pl.BlockSpec((1, npc, V_HEADS, chunk), lambda i, j: (i, j, 0, 0))
    state = pl.BlockSpec((1, V_HEADS, HEAD_DIM, HEAD_DIM), lambda i, j: (i, 0, 0, 0))
    return pl.pallas_call(
        functools.partial(_gdn_core_kernel, chunk=chunk),
        grid=(b, nt),
        in_specs=[tok(D_QK), tok(D_QK), tok(D_V), rows, rows, state, _const_spec((1, HEAD_DIM))],
        out_specs=[tok(D_V), state],
        out_shape=[jax.ShapeDtypeStruct((b, t_len, D_V), F32),
                   jax.ShapeDtypeStruct((b, V_HEADS, HEAD_DIM, HEAD_DIM), F32)],
        scratch_shapes=[pltpu.VMEM((npc * V_HEADS, chunk, chunk), F32),
                        pltpu.VMEM((npc * V_HEADS, chunk, chunk), BF16),
                        pltpu.VMEM((npc * V_HEADS, chunk, chunk), BF16),
                        pltpu.VMEM((npc * V_HEADS, chunk, chunk), BF16),
                        pltpu.VMEM((npc * V_HEADS, HEAD_DIM, chunk), BF16),
                        pltpu.VMEM((npc, chunk, 3 * V_HEADS), F32),
                        pltpu.VMEM((K_HEADS, 2 * chunk, 2 * HEAD_DIM), F32),
                        pltpu.VMEM((V_HEADS, chunk, HEAD_DIM), BF16)],
        compiler_params=_params(2),
        name="gdn_core",
    )(q, k, v, beta_t, g_t, s0, w_onorm.reshape(1, HEAD_DIM))


def _gdn_tail_kernel(o_ref, z_ref, x_ref, wout_ref, nw_ref, wg_ref, wu_ref, wd_ref, nf_ref,
                     y_ref, gated_ref, mid_ref, hb_ref, act_ref):
    for c in range(0, D_V, COL_BLOCK):
        cs = slice(c, c + COL_BLOCK)
        gated_ref[:, cs] = (o_ref[:, cs] * _silu(z_ref[:, cs])).astype(BF16)
    _residual_dot_store(x_ref, gated_ref, wout_ref, mid_ref)
    _run(_ffn_steps(mid_ref, nw_ref, wg_ref, wu_ref, wd_ref, hb_ref, act_ref, [y_ref], nf_ref))


def _gdn_tail(o, z, x, w_out, norm_w, w_gate, w_up, w_down, norm_final, *, tb):
    n = x.shape[0]
    consts = (_mxu_weight(w_out), norm_w.reshape(1, D_MODEL), _mxu_weight(w_gate), _mxu_weight(w_up),
              _mxu_weight(w_down), norm_final.reshape(1, D_MODEL))
    return pl.pallas_call(
        _gdn_tail_kernel,
        grid=(n // tb,),
        in_specs=[pl.BlockSpec((tb, D_V), lambda i: (i, 0)), pl.BlockSpec((tb, D_V), lambda i: (i, 0)),
                  pl.BlockSpec((tb, D_MODEL), lambda i: (i, 0))] + [_const_spec(a.shape) for a in consts],
        out_specs=pl.BlockSpec((tb, D_MODEL), lambda i: (i, 0)),
        out_shape=jax.ShapeDtypeStruct((n, D_MODEL), F32),
        scratch_shapes=[pltpu.VMEM((tb, D_V), BF16), pltpu.VMEM((tb, D_MODEL), F32),
                        pltpu.VMEM((tb, D_MODEL), BF16), pltpu.VMEM((tb, D_FF), BF16)],
        compiler_params=_params(1),
        name="gdn_out_ffn",
    )(o, z, x, *consts)


def _trunk(x, conv_buf, s0, p, *, sgu_chunk, gdn_chunk, tb, sgu_tb, gdn_tb, core_tb, write_v):
    b, t_len, _ = x.shape
    flat = lambda a: a.reshape(b * t_len, a.shape[-1])
    x0, v_rows = _sgu_layer(flat(x), p["norm_mix"][0], p["sgu_w_in"][0], p["sgu_ln_g"][0],
                            p["sgu_ln_b"][0], p["sgu_w_s"][0], p["sgu_b_s"][0], p["sgu_w_out"][0],
                            chunk=sgu_chunk, tb=sgu_tb, write_v=write_v)
    ffn_w = (p["norm_ffn"][0], p["ffn_w_gate"][0], p["ffn_w_up"][0], p["ffn_w_down"][0])
    proj_w = (p["norm_mix"][1], p["gdn_w_in"][0], p["gdn_w_conv"][0], p["gdn_a_log"][0], p["gdn_dt_bias"][0])
    if t_len > gdn_tb:
        x1, *proj = _ffn_proj(x0, conv_buf, *ffn_w, *proj_w, t_len=t_len, chunk=gdn_chunk, tb=gdn_tb)
    else:
        x1 = _ffn_layer(x0, *ffn_w, tb=tb)
        proj = _gdn_proj(x1, conv_buf, *proj_w, t_len=t_len, chunk=gdn_chunk, tb=gdn_tb)
    q, k, v, z, beta_t, g_t = (a.reshape(b, a.shape[0] // b, *a.shape[1:]) for a in proj[:-1])
    conv_tail = proj[-1]
    o, s_new = _gdn_core(q, k, v, beta_t, g_t, s0, p["gdn_w_onorm"][0], chunk=gdn_chunk, tb=core_tb)
    y = _gdn_tail(flat(o), flat(z), x1, p["gdn_w_out"][0], p["norm_ffn"][1], p["ffn_w_gate"][1],
                  p["ffn_w_up"][1], p["ffn_w_down"][1], p["norm_final"], tb=tb)
    conv_new = conv_tail[:, SUBLANES - (CONV_K - 1):, :]
    if write_v:
        v_rows = v_rows.reshape(b, t_len, D_SGU)
    return y.reshape(b, t_len, D_MODEL), s_new, conv_new, v_rows


def kernel(x_prompt, x_sample, state_gdn, state_conv, norm_mix, norm_ffn, norm_final, sgu_w_in, sgu_ln_g, sgu_ln_b, sgu_w_s, sgu_b_s, sgu_w_out, gdn_w_in, gdn_w_conv, gdn_a_log, gdn_dt_bias, gdn_w_onorm, gdn_w_out, ffn_w_gate, ffn_w_up, ffn_w_down):
    p = dict(norm_mix=norm_mix, norm_ffn=norm_ffn, norm_final=norm_final, sgu_w_in=sgu_w_in,
             sgu_ln_g=sgu_ln_g, sgu_ln_b=sgu_ln_b, sgu_w_s=sgu_w_s, sgu_b_s=sgu_b_s,
             sgu_w_out=sgu_w_out, gdn_w_in=gdn_w_in, gdn_w_conv=gdn_w_conv, gdn_a_log=gdn_a_log,
             gdn_dt_bias=gdn_dt_bias, gdn_w_onorm=gdn_w_onorm, gdn_w_out=gdn_w_out,
             ffn_w_gate=ffn_w_gate, ffn_w_up=ffn_w_up, ffn_w_down=ffn_w_down)
    bp = x_prompt.shape[0]
    dec_seq = x_sample.shape[1]
    zero_conv = jnp.zeros((bp, CONV_K - 1, D_QKV), F32)
    zero_state = jnp.zeros((bp, V_HEADS, HEAD_DIM, HEAD_DIM), F32)
    yp, sp, cp, _ = _trunk(x_prompt, zero_conv, zero_state, p, sgu_chunk=SGU_CHUNK,
                           gdn_chunk=2 * GDN_CHUNK, tb=256, sgu_tb=512, gdn_tb=256, core_tb=512, write_v=False)
    ys, ss, cs, vs = _trunk(x_sample, state_conv[0], state_gdn[0], p, sgu_chunk=dec_seq,
                            gdn_chunk=dec_seq, tb=256, sgu_tb=256, gdn_tb=256, core_tb=dec_seq, write_v=True)
    return (yp, ys, sp[None], cp[None], ss[None], cs[None], vs[None])
```

```python
import functools
import math
from typing import NamedTuple

import jax
import jax.numpy as jnp
from jax import lax
from jax.experimental import pallas as pl
from jax.experimental.pallas import tpu as pltpu

D_MODEL = 1024
D_SGU = 2048
SGU_GROUPS = 8
SGU_GROUP_DIM = D_SGU // SGU_GROUPS
SGU_CHUNK = 128
HEAD_DIM = 128
K_HEADS = 8
V_HEADS = 16
D_QK = K_HEADS * HEAD_DIM
D_V = V_HEADS * HEAD_DIM
D_QKV = 2 * D_QK + D_V
CONV_K = 4
GDN_CHUNK = 64
D_FF = 2816
EPS = 1e-6
LN_EPS = 1e-5

VMEM_LIMIT_BYTES = 56 * 1024 * 1024
SUBLANES = 8
COL_BLOCK = 256
LANE_PAIR_TILE = 512
INV_BASE = 8

F32 = jnp.float32
BF16 = jnp.bfloat16
HIGHEST = lax.Precision.HIGHEST


class _Layer(NamedTuple):
    stacked: jax.Array
    index: int


def _const_spec(a):
    if isinstance(a, _Layer):
        index = a.index
        return pl.BlockSpec((None,) + a.stacked.shape[1:], lambda *_: (index, 0, 0),
                            pipeline_mode=pl.Buffered(1))
    nd = a.ndim
    return pl.BlockSpec(a.shape, lambda *_: (0,) * nd, pipeline_mode=pl.Buffered(1))


def _operands(consts):
    return [a.stacked if isinstance(a, _Layer) else a for a in consts]


def _mxu_weight(w):
    if isinstance(w, _Layer):
        return w
    wb = w.astype(BF16)
    if w.shape[-1] % LANE_PAIR_TILE == 0:
        wb = jnp.pad(wb, ((0, 0),) * (w.ndim - 1) + ((0, COL_BLOCK),))
    return wb


def _params(n_grid):
    return pltpu.CompilerParams(dimension_semantics=("arbitrary",) * n_grid,
                                vmem_limit_bytes=VMEM_LIMIT_BYTES)


def _rms_scale(x):
    return lax.rsqrt(jnp.mean(x * x, axis=-1, keepdims=True) + EPS)


def _silu(x):
    return x * jax.nn.sigmoid(x)


def _gelu_tanh(x):
    return 0.5 * x * (1.0 + jnp.tanh(math.sqrt(2.0 / math.pi) * (x + 0.044715 * (x * x * x))))


def _softplus(x):
    return jnp.maximum(x, 0.0) + jnp.log1p(jnp.exp(-jnp.abs(x)))


def _dot(a, b):
    return jnp.dot(a, b, preferred_element_type=F32)


def _dot_nt(a, b, precision=None):
    return lax.dot_general(a, b, (((1,), (1,)), ((), ())), precision=precision,
                           preferred_element_type=F32)


def _dot_tn(a, b):
    return lax.dot_general(a, b, (((0,), (0,)), ((), ())), preferred_element_type=F32)


def _residual_dot_store(x_ref, a_ref, w_ref, o_ref):
    ssq = jnp.zeros((x_ref.shape[0], 1), F32)
    for c in range(0, x_ref.shape[1], COL_BLOCK):
        cs = slice(c, c + COL_BLOCK)
        y = x_ref[:, cs] + _dot(a_ref[...], w_ref[:, cs])
        o_ref[:, cs] = y
        ssq = ssq + jnp.sum(y * y, axis=-1, keepdims=True)
    return ssq


def _run(steps):
    for step in steps:
        step()


def _interleave(a_steps, b_steps):
    keyed = [((i + 0.5) / len(a_steps), 0, s) for i, s in enumerate(a_steps)]
    keyed += [((j + 0.5) / len(b_steps), 1, s) for j, s in enumerate(b_steps)]
    return [s for _, _, s in sorted(keyed, key=lambda t: t[:2])]


def _ffn_steps(x_ref, nw_ref, wg_ref, wu_ref, wd_ref, hb_ref, act_ref, o_refs, nf_ref=None):
    def norm():
        x = x_ref[...]
        hb_ref[...] = (x * _rms_scale(x) * nw_ref[...]).astype(BF16)

    def gate_up(c):
        g = _dot(hb_ref[...], wg_ref[:, c:c + COL_BLOCK])
        u = _dot(hb_ref[...], wu_ref[:, c:c + COL_BLOCK])
        act_ref[:, c:c + COL_BLOCK] = (_silu(g) * u).astype(BF16)

    ssq = [jnp.zeros((x_ref.shape[0], 1), F32)]

    def down(c):
        cs = slice(c, c + COL_BLOCK)
        y = x_ref[:, cs] + _dot(act_ref[...], wd_ref[:, cs])
        for o_ref in o_refs:
            o_ref[:, cs] = y
        if nf_ref is not None:
            ssq[0] = ssq[0] + jnp.sum(y * y, axis=-1, keepdims=True)

    def final_norm():
        for o_ref in o_refs:
            o_ref[...] = o_ref[...] * lax.rsqrt(ssq[0] * (1.0 / D_MODEL) + EPS) * nf_ref[...]

    steps = [norm] + [functools.partial(gate_up, c) for c in range(0, D_FF, COL_BLOCK)]
    steps += [functools.partial(down, c) for c in range(0, D_MODEL, COL_BLOCK)]
    return steps + ([final_norm] if nf_ref is not None else [])


def _sgu_kernel(x_ref, nw_ref, win_ref, lng_ref, lnb_ref, ws_ref, bs_ref, wout_ref,
                *rest, chunk, write_v):
    if write_v:
        o_ref, v_ref, hb_ref, u_ref, vv_ref, gated_ref = rest
    else:
        o_ref, hb_ref, u_ref, vv_ref, gated_ref = rest
        v_ref = None
    tb = x_ref.shape[0]
    x = x_ref[...]
    hb_ref[...] = (x * _rms_scale(x) * nw_ref[...]).astype(BF16)

    vsum = jnp.zeros((tb, 1), F32)
    for c in range(0, 2 * D_SGU, COL_BLOCK):
        uv = _gelu_tanh(_dot(hb_ref[...], win_ref[:, c:c + COL_BLOCK]))
        if c < D_SGU:
            u_ref[:, c:c + COL_BLOCK] = uv
        else:
            vv_ref[:, c - D_SGU:c - D_SGU + COL_BLOCK] = uv
            vsum = vsum + jnp.sum(uv, axis=-1, keepdims=True)
    mu = vsum * (1.0 / D_SGU)
    vvar = jnp.zeros((tb, 1), F32)
    for c in range(0, D_SGU, COL_BLOCK):
        d = vv_ref[:, c:c + COL_BLOCK] - mu
        vvar = vvar + jnp.sum(d * d, axis=-1, keepdims=True)
    rstd = lax.rsqrt(vvar * (1.0 / D_SGU) + LN_EPS)

    row = lax.broadcasted_iota(jnp.int32, (chunk, chunk), 0)
    col = lax.broadcasted_iota(jnp.int32, (chunk, chunk), 1)
    causal = row >= col
    for g in range(SGU_GROUPS):
        c0 = g * SGU_GROUP_DIM
        cs = slice(c0, c0 + SGU_GROUP_DIM)
        vn = (vv_ref[:, cs] - mu) * rstd * lng_ref[:, cs] + lnb_ref[:, cs]
        if write_v:
            v_ref[:, cs] = vn
        vnb = vn.astype(BF16)
        wsg = jnp.where(causal, ws_ref[g], 0.0).astype(BF16)
        bias = bs_ref[:, g:g + 1]
        for r in range(0, tb, chunk):
            mixed = _dot(wsg, vnb[r:r + chunk]) + bias
            gated_ref[r:r + chunk, cs] = (u_ref[r:r + chunk, cs] * mixed).astype(BF16)
    _residual_dot_store(x_ref, gated_ref, wout_ref, o_ref)


def _sgu_layer(x, norm_w, w_in, ln_g, ln_b, w_s, b_s, w_out, *, chunk, tb, write_v):
    n = x.shape[0]
    ws = w_s[:, :chunk, :chunk]
    bs_t = b_s[:, :chunk].T
    out_shape = [jax.ShapeDtypeStruct((n, D_MODEL), F32)]
    out_specs = [pl.BlockSpec((tb, D_MODEL), lambda i: (i, 0))]
    if write_v:
        out_shape.append(jax.ShapeDtypeStruct((n, D_SGU), F32))
        out_specs.append(pl.BlockSpec((tb, D_SGU), lambda i: (i, 0)))
    consts = (norm_w.reshape(1, D_MODEL), _mxu_weight(w_in), ln_g.reshape(1, D_SGU),
              ln_b.reshape(1, D_SGU), ws, bs_t, _mxu_weight(w_out))
    res = pl.pallas_call(
        functools.partial(_sgu_kernel, chunk=chunk, write_v=write_v),
        grid=(n // tb,),
        in_specs=[pl.BlockSpec((tb, D_MODEL), lambda i: (i, 0))] + [_const_spec(a) for a in consts],
        out_specs=out_specs,
        out_shape=out_shape,
        scratch_shapes=[
            pltpu.VMEM((tb, D_MODEL), BF16),
            pltpu.VMEM((tb, D_SGU), F32),
            pltpu.VMEM((tb, D_SGU), F32),
            pltpu.VMEM((tb, D_SGU), BF16),
        ],
        compiler_params=_params(1),
        name="sgu_mixer",
    )(x, *_operands(consts))
    return res if write_v else (res[0], None)


def _ffn_kernel(x_ref, nw_ref, wg_ref, wu_ref, wd_ref, o_ref, hb_ref, act_ref):
    _run(_ffn_steps(x_ref, nw_ref, wg_ref, wu_ref, wd_ref, hb_ref, act_ref, [o_ref]))


def _ffn_layer(x, norm_w, w_gate, w_up, w_down, *, tb):
    n = x.shape[0]
    consts = (norm_w.reshape(1, D_MODEL), _mxu_weight(w_gate), _mxu_weight(w_up), _mxu_weight(w_down))
    return pl.pallas_call(
        _ffn_kernel,
        grid=(n // tb,),
        in_specs=[pl.BlockSpec((tb, D_MODEL), lambda i: (i, 0))] + [_const_spec(a) for a in consts],
        out_specs=pl.BlockSpec((tb, D_MODEL), lambda i: (i, 0)),
        out_shape=jax.ShapeDtypeStruct((n, D_MODEL), F32),
        scratch_shapes=[pltpu.VMEM((tb, D_MODEL), BF16), pltpu.VMEM((tb, D_FF), BF16)],
        compiler_params=_params(1),
        name="ffn",
    )(x, *_operands(consts))


def _proj_steps(x_ref, hb_ref, hist_refs, carry, nw_ref, wqkv_ref, wz_ref, wbat_ref, wconv_ref, alog_ref,
                dtb_ref, q_ref, k_ref, v_ref, z_ref, beta_ref, g_ref, cnew_refs, *, chunk):
    tb = x_ref.shape[0]
    seg = tb // len(hist_refs)

    def norm():
        x = x_ref[...]
        hb_ref[...] = (x * _rms_scale(x) * nw_ref[...]).astype(BF16)

    def conv_block(c):
        cs = slice(c, c + COL_BLOCK)
        pre_all = _dot(hb_ref[...], wqkv_ref[:, cs])
        for s, (hist_ref, cnew_ref) in enumerate(zip(hist_refs, cnew_refs)):
            rs = slice(s * seg, (s + 1) * seg)
            pre = pre_all[rs]
            padded = jnp.concatenate([hist_ref[:, cs], pre], axis=0)
            if carry:
                hist_ref[:, cs] = pre[seg - SUBLANES:, :]
            cnew_ref[:, cs] = pre[seg - SUBLANES:, :]
            acc = padded * wconv_ref[0:1, cs]
            for i in range(1, CONV_K):
                acc = pltpu.roll(acc, 1, axis=0) + padded * wconv_ref[i:i + 1, cs]
            y = _silu(acc[SUBLANES:, :])
            if c < 2 * D_QK:
                dst, c_dst, scale = (q_ref, c, HEAD_DIM ** -0.5) if c < D_QK else (k_ref, c - D_QK, 1.0)
                for h in range(0, COL_BLOCK, HEAD_DIM):
                    yh = y[:, h:h + HEAD_DIM]
                    inv = lax.rsqrt(jnp.sum(yh * yh, axis=-1, keepdims=True) + EPS) * scale
                    dst[rs, c_dst + h:c_dst + h + HEAD_DIM] = (yh * inv).astype(BF16)
            else:
                v_ref[rs, c - 2 * D_QK:c - 2 * D_QK + COL_BLOCK] = y
        blk = c // COL_BLOCK
        if blk % 2 == 1:
            cz = (blk // 2) * COL_BLOCK
            z_ref[:, cz:cz + COL_BLOCK] = _dot(hb_ref[...], wz_ref[:, cz:cz + COL_BLOCK])

    def gates():
        ba_t = _dot_nt(wbat_ref[...], hb_ref[...])
        beta_t = jax.nn.sigmoid(ba_t[:V_HEADS])
        loga_t = -jnp.exp(alog_ref[...]) * _softplus(ba_t[V_HEADS:] + dtb_ref[...])
        row = lax.broadcasted_iota(jnp.int32, (chunk, chunk), 0)
        col = lax.broadcasted_iota(jnp.int32, (chunk, chunk), 1)
        upper = (row <= col).astype(F32)
        for j in range(tb // chunk):
            js = slice(j * chunk, (j + 1) * chunk)
            beta_ref[j] = beta_t[:, js]
            g_ref[j] = jnp.dot(loga_t[:, js], upper, precision=HIGHEST, preferred_element_type=F32)

    return [norm] + [functools.partial(conv_block, c) for c in range(0, D_QKV, COL_BLOCK)] + [gates]


def _gdn_proj_kernel(x_ref, cbuf_ref, nw_ref, wqkv_ref, wz_ref, wba_ref, wconv_ref,
                     alog_ref, dtb_ref,
                     q_ref, k_ref, v_ref, z_ref, beta_ref, g_ref, cnew_ref,
                     hb_ref, wbat_ref, *, chunk):
    @pl.when(pl.program_id(0) == 0)
    def _():
        wbat_ref[...] = wba_ref[...].T.astype(BF16)

    n_streams = cbuf_ref.shape[0]
    _run(_proj_steps(x_ref, hb_ref, [cbuf_ref.at[s] for s in range(n_streams)], False,
                     nw_ref, wqkv_ref, wz_ref, wbat_ref, wconv_ref, alog_ref, dtb_ref,
                     q_ref, k_ref, v_ref, z_ref, beta_ref, g_ref,
                     [cnew_ref.at[s] for s in range(n_streams)], chunk=chunk))


def _ffn_proj_kernel(x0_ref, cbuf_ref, nwf_ref, wg_ref, wu_ref, wd_ref,
                     nwp_ref, wqkv_ref, wz_ref, wba_ref, wconv_ref, alog_ref, dtb_ref,
                     x1_ref, q_ref, k_ref, v_ref, z_ref, beta_ref, g_ref, cnew_ref,
                     x1s_ref, hbf_ref, act_ref, hbp_ref, hist_ref, wbat_ref, *, chunk, tiles_per_stream):
    i = pl.program_id(0)

    @pl.when(i == 0)
    def _():
        x1s_ref[...] = jnp.zeros_like(x1s_ref)
        hist_ref[...] = jnp.zeros_like(hist_ref)
        wbat_ref[...] = wba_ref[...].T.astype(BF16)

    @pl.when((i >= 1) & ((i - 1) % tiles_per_stream == 0))
    def _():
        hist_ref[...] = cbuf_ref[0]

    ffn = _ffn_steps(x0_ref, nwf_ref, wg_ref, wu_ref, wd_ref, hbf_ref, act_ref, [x1_ref, x1s_ref])
    proj = _proj_steps(x1s_ref, hbp_ref, [hist_ref], True, nwp_ref, wqkv_ref, wz_ref, wbat_ref, wconv_ref,
                       alog_ref, dtb_ref, q_ref, k_ref, v_ref, z_ref, beta_ref, g_ref, [cnew_ref.at[0]],
                       chunk=chunk)
    proj[0]()
    _run(_interleave(ffn, proj[1:]))


def _gdn_proj(x, conv_buf, norm_w, w_in, w_conv, a_log, dt_bias, *, t_len, chunk, tb):
    n = x.shape[0]
    spt = tb // t_len
    npc = tb // chunk
    cbuf = jnp.pad(conv_buf, ((0, 0), (SUBLANES - (CONV_K - 1), 0), (0, 0)))
    consts = (norm_w.reshape(1, D_MODEL), _mxu_weight(w_in[:, :D_QKV]),
              _mxu_weight(w_in[:, D_QKV:D_QKV + D_V]), w_in[:, D_QKV + D_V:], w_conv,
              a_log.reshape(V_HEADS, 1), dt_bias.reshape(V_HEADS, 1))
    tok = lambda width: pl.BlockSpec((tb, width), lambda i: (i, 0))
    rows = pl.BlockSpec((npc, V_HEADS, chunk), lambda i: (i, 0, 0))
    conv_state = pl.BlockSpec((spt, SUBLANES, D_QKV), lambda i: (i, 0, 0))
    return pl.pallas_call(
        functools.partial(_gdn_proj_kernel, chunk=chunk),
        grid=(n // tb,),
        in_specs=[tok(D_MODEL), conv_state] + [_const_spec(a) for a in consts],
        out_specs=[tok(D_QK), tok(D_QK), tok(D_V), tok(D_V), rows, rows, conv_state],
        out_shape=[
            jax.ShapeDtypeStruct((n, D_QK), BF16),
            jax.ShapeDtypeStruct((n, D_QK), BF16),
            jax.ShapeDtypeStruct((n, D_V), F32),
            jax.ShapeDtypeStruct((n, D_V), F32),
            jax.ShapeDtypeStruct((n // chunk, V_HEADS, chunk), F32),
            jax.ShapeDtypeStruct((n // chunk, V_HEADS, chunk), F32),
            jax.ShapeDtypeStruct((n // t_len, SUBLANES, D_QKV), F32),
        ],
        scratch_shapes=[pltpu.VMEM((tb, D_MODEL), BF16), pltpu.VMEM((2 * V_HEADS, D_MODEL), BF16)],
        compiler_params=_params(1),
        name="gdn_proj",
    )(x, cbuf, *_operands(consts))


def _ffn_proj(x0, conv_buf, ffn_norm_w, w_gate, w_up, w_down, norm_w, w_in, w_conv, a_log, dt_bias,
              *, t_len, chunk, tb):
    n = x0.shape[0]
    n_tiles = n // tb
    tiles_per_stream = t_len // tb
    npc = tb // chunk
    cbuf = jnp.pad(conv_buf, ((0, 0), (SUBLANES - (CONV_K - 1), 0), (0, 0)))
    consts = (ffn_norm_w.reshape(1, D_MODEL), _mxu_weight(w_gate), _mxu_weight(w_up), _mxu_weight(w_down),
              norm_w.reshape(1, D_MODEL), _mxu_weight(w_in[:, :D_QKV]),
              _mxu_weight(w_in[:, D_QKV:D_QKV + D_V]), w_in[:, D_QKV + D_V:], w_conv,
              a_log.reshape(V_HEADS, 1), dt_bias.reshape(V_HEADS, 1))
    ffn_tile = lambda i: jnp.minimum(i, n_tiles - 1)
    proj_tile = lambda i: jnp.maximum(i - 1, 0)
    tok = lambda width: pl.BlockSpec((tb, width), lambda i: (proj_tile(i), 0))
    rows = pl.BlockSpec((npc, V_HEADS, chunk), lambda i: (proj_tile(i), 0, 0))
    conv_state = pl.BlockSpec((1, SUBLANES, D_QKV), lambda i: (proj_tile(i) // tiles_per_stream, 0, 0))
    return pl.pallas_call(
        functools.partial(_ffn_proj_kernel, chunk=chunk, tiles_per_stream=tiles_per_stream),
        grid=(n_tiles + 1,),
        in_specs=[pl.BlockSpec((tb, D_MODEL), lambda i: (ffn_tile(i), 0)), conv_state]
        + [_const_spec(a) for a in consts],
        out_specs=[pl.BlockSpec((tb, D_MODEL), lambda i: (ffn_tile(i), 0)),
                   tok(D_QK), tok(D_QK), tok(D_V), tok(D_V), rows, rows, conv_state],
        out_shape=[
            jax.ShapeDtypeStruct((n, D_MODEL), F32),
            jax.ShapeDtypeStruct((n, D_QK), BF16),
            jax.ShapeDtypeStruct((n, D_QK), BF16),
            jax.ShapeDtypeStruct((n, D_V), F32),
            jax.ShapeDtypeStruct((n, D_V), F32),
            jax.ShapeDtypeStruct((n // chunk, V_HEADS, chunk), F32),
            jax.ShapeDtypeStruct((n // chunk, V_HEADS, chunk), F32),
            jax.ShapeDtypeStruct((n // t_len, SUBLANES, D_QKV), F32),
        ],
        scratch_shapes=[pltpu.VMEM((tb, D_MODEL), F32), pltpu.VMEM((tb, D_MODEL), BF16),
                        pltpu.VMEM((tb, D_FF), BF16), pltpu.VMEM((tb, D_MODEL), BF16),
                        pltpu.VMEM((SUBLANES, D_QKV), F32), pltpu.VMEM((2 * V_HEADS, D_MODEL), BF16)],
        compiler_params=_params(1),
        name="ffn_gdn_proj",
    )(x0, cbuf, *_operands(consts))


def _gdn_core_kernel(q_ref, k_ref, v_ref, beta_ref, g_ref, s0_ref, won_ref, o_ref, s_ref,
                     t_ref, p_ref, x_ref, attn_ref, kdt_ref, col_ref, ks_ref, dl_ref, *, chunk):
    @pl.when(pl.program_id(1) == 0)
    def _():
        s_ref[...] = s0_ref[...]

    n_chunks = q_ref.shape[1] // chunk
    row = lax.broadcasted_iota(jnp.int32, (chunk, chunk), 0)
    col = lax.broadcasted_iota(jnp.int32, (chunk, chunk), 1)
    incl = row >= col
    strict = row > col
    eye = (row == col).astype(F32)
    diag_blk = (row // INV_BASE) == (col // INV_BASE)
    base_factors = INV_BASE.bit_length() - 1
    slot = lambda c, h: c * V_HEADS + h

    def kq_of(c, pr):
        rs = slice(c * chunk, (c + 1) * chunk)
        hs = slice(pr * HEAD_DIM, (pr + 1) * HEAD_DIM)
        return jnp.concatenate([k_ref[0, rs, hs], q_ref[0, rs, hs]], axis=0)

    for c in range(n_chunks):
        g_rows = g_ref[0, c]
        rows_t = jnp.concatenate([beta_ref[0, c], g_rows, jnp.exp(g_rows)], axis=0)
        col_ref[c] = rows_t.T

    def build_operands(c, pr):
        kq = kq_of(c, pr)
        kq_kt = _dot_nt(kq, kq[:chunk])
        k_t = kq[:chunk].astype(F32).T
        for h in (2 * pr, 2 * pr + 1):
            g_row = g_ref[0, c, h:h + 1, :]
            k_dec = jnp.exp(g_row[:, chunk - 1:chunk] - g_row)
            kdt_ref[slot(c, h)] = (k_t * k_dec).astype(BF16)
            b_col = col_ref[c, :, h:h + 1]
            g_col = col_ref[c, :, V_HEADS + h:V_HEADS + h + 1]
            decay = jnp.where(incl, jnp.exp(jnp.where(incl, g_col - g_row, 0.0)), 0.0)
            x_mat = jnp.where(strict, -(b_col * kq_kt[:chunk] * decay), 0.0)
            x_diag = jnp.where(diag_blk, x_mat, 0.0)
            x_ref[slot(c, h)] = x_mat.astype(BF16)
            t_ref[slot(c, h)] = eye + x_diag
            p_ref[slot(c, h)] = x_diag.astype(BF16)
            attn_ref[slot(c, h)] = jnp.where(incl, kq_kt[chunk:] * decay, 0.0).astype(BF16)

    def square_round(c):
        for h in range(V_HEADS):
            xd = p_ref[slot(c, h)]
            p_ref[slot(c, h)] = _dot(xd, xd).astype(BF16)

    def base_round(c, j):
        for h in range(V_HEADS):
            pb = p_ref[slot(c, h)]
            tm = t_ref[slot(c, h)]
            if j < base_factors - 1:
                pt = _dot(pb, jnp.concatenate([pb, tm.astype(BF16)], axis=1))
                p_ref[slot(c, h)] = pt[:, :chunk].astype(BF16)
                t_ref[slot(c, h)] = tm + pt[:, chunk:]
            else:
                t_ref[slot(c, h)] = tm + _dot(pb, tm.astype(BF16))

    def widen_round(c, width):
        off = ((row // (2 * width)) == (col // (2 * width))) & ((row // width) != (col // width))
        for h in range(V_HEADS):
            tb16 = t_ref[slot(c, h)].astype(BF16)
            w_mat = _dot(jnp.where(off, x_ref[slot(c, h)], jnp.zeros((), BF16)), tb16)
            t_new = t_ref[slot(c, h)] + _dot(tb16, w_mat.astype(BF16))
            if 2 * width < chunk:
                t_ref[slot(c, h)] = t_new
            else:
                p_ref[slot(c, h)] = t_new.astype(BF16)

    def inverse_rounds(c):
        rounds = [functools.partial(square_round, c)]
        rounds += [functools.partial(base_round, c, j) for j in range(1, base_factors)]
        width = INV_BASE
        while width < chunk:
            rounds.append(functools.partial(widen_round, c, width))
            width *= 2
        return rounds

    for c in range(n_chunks):
        for pr in range(K_HEADS):
            build_operands(c, pr)
    for same_round in zip(*[inverse_rounds(c) for c in range(n_chunks)]):
        _run(same_round)

    for c in range(n_chunks):
        rs = slice(c * chunk, (c + 1) * chunk)
        for pr in range(K_HEADS):
            s_pair = jnp.concatenate([s_ref[0, 2 * pr], s_ref[0, 2 * pr + 1]], axis=1).astype(BF16)
            ks_ref[pr] = _dot(kq_of(c, pr), s_pair)
        for h in range(V_HEADS):
            es = slice((h % 2) * HEAD_DIM, (h % 2 + 1) * HEAD_DIM)
            b_col = col_ref[c, :, h:h + 1]
            gam = col_ref[c, :, 2 * V_HEADS + h:2 * V_HEADS + h + 1]
            rhs = b_col * (v_ref[0, rs, h * HEAD_DIM:(h + 1) * HEAD_DIM] - gam * ks_ref[h // 2, :chunk, es])
            dl_ref[h] = _dot(p_ref[slot(c, h)], rhs.astype(BF16)).astype(BF16)
        for h in range(V_HEADS):
            s_decay = jnp.exp(g_ref[0, c, h:h + 1, chunk - 1:chunk])
            s_ref[0, h] = s_decay * s_ref[0, h] + _dot(kdt_ref[slot(c, h)], dl_ref[h])
        for h in range(V_HEADS):
            es = slice((h % 2) * HEAD_DIM, (h % 2 + 1) * HEAD_DIM)
            gam = col_ref[c, :, 2 * V_HEADS + h:2 * V_HEADS + h + 1]
            o = gam * ks_ref[h // 2, chunk:, es] + _dot(attn_ref[slot(c, h)], dl_ref[h])
            o_ref[0, rs, h * HEAD_DIM:(h + 1) * HEAD_DIM] = o * _rms_scale(o) * won_ref[...]


def _gdn_core(q, k, v, beta_t, g_t, s0, w_onorm, *, chunk, tb):
    b, t_len, _ = q.shape
    nt = t_len // tb
    npc = tb // chunk
    tok = lambda width: pl.BlockSpec((1, tb, width), lambda i, j: (i, j, 0))
    rows = pl.BlockSpec((1, npc, V_HEADS, chunk), lambda i, j: (i, j, 0, 0))
    state = pl.BlockSpec((1, V_HEADS, HEAD_DIM, HEAD_DIM), lambda i, j: (i, 0, 0, 0))
    return pl.pallas_call(
        functools.partial(_gdn_core_kernel, chunk=chunk),
        grid=(b, nt),
        in_specs=[tok(D_QK), tok(D_QK), tok(D_V), rows, rows, state, _const_spec(w_onorm.reshape(1, HEAD_DIM))],
        out_specs=[tok(D_V), state],
        out_shape=[jax.ShapeDtypeStruct((b, t_len, D_V), F32),
                   jax.ShapeDtypeStruct((b, V_HEADS, HEAD_DIM, HEAD_DIM), F32)],
        scratch_shapes=[pltpu.VMEM((npc * V_HEADS, chunk, chunk), F32),
                        pltpu.VMEM((npc * V_HEADS, chunk, chunk), BF16),
                        pltpu.VMEM((npc * V_HEADS, chunk, chunk), BF16),
                        pltpu.VMEM((npc * V_HEADS, chunk, chunk), BF16),
                        pltpu.VMEM((npc * V_HEADS, HEAD_DIM, chunk), BF16),
                        pltpu.VMEM((npc, chunk, 3 * V_HEADS), F32),
                        pltpu.VMEM((K_HEADS, 2 * chunk, 2 * HEAD_DIM), F32),
                        pltpu.VMEM((V_HEADS, chunk, HEAD_DIM), BF16)],
        compiler_params=_params(2),
        name="gdn_core",
    )(q, k, v, beta_t, g_t, s0, w_onorm.reshape(1, HEAD_DIM))


def _gdn_tail_kernel(o_ref, z_ref, x_ref, wout_ref, nw_ref, wg_ref, wu_ref, wd_ref, nf_ref,
                     y_ref, gated_ref, mid_ref, hb_ref, act_ref):
    for c in range(0, D_V, COL_BLOCK):
        cs = slice(c, c + COL_BLOCK)
        gated_ref[:, cs] = (o_ref[:, cs] * _silu(z_ref[:, cs])).astype(BF16)
    _residual_dot_store(x_ref, gated_ref, wout_ref, mid_ref)
    _run(_ffn_steps(mid_ref, nw_ref, wg_ref, wu_ref, wd_ref, hb_ref, act_ref, [y_ref], nf_ref))


def _gdn_tail(o, z, x, w_out, norm_w, w_gate, w_up, w_down, norm_final, *, tb):
    n = x.shape[0]
    consts = (_mxu_weight(w_out), norm_w.reshape(1, D_MODEL), _mxu_weight(w_gate), _mxu_weight(w_up),
              _mxu_weight(w_down), norm_final.reshape(1, D_MODEL))
    return pl.pallas_call(
        _gdn_tail_kernel,
        grid=(n // tb,),
        in_specs=[pl.BlockSpec((tb, D_V), lambda i: (i, 0)), pl.BlockSpec((tb, D_V), lambda i: (i, 0)),
                  pl.BlockSpec((tb, D_MODEL), lambda i: (i, 0))] + [_const_spec(a) for a in consts],
        out_specs=pl.BlockSpec((tb, D_MODEL), lambda i: (i, 0)),
        out_shape=jax.ShapeDtypeStruct((n, D_MODEL), F32),
        scratch_shapes=[pltpu.VMEM((tb, D_V), BF16), pltpu.VMEM((tb, D_MODEL), F32),
                        pltpu.VMEM((tb, D_MODEL), BF16), pltpu.VMEM((tb, D_FF), BF16)],
        compiler_params=_params(1),
        name="gdn_out_ffn",
    )(o, z, x, *_operands(consts))


def _trunk(x, conv_buf, s0, p, *, sgu_chunk, gdn_chunk, tb, sgu_tb, gdn_tb, core_tb, write_v):
    b, t_len, _ = x.shape
    flat = lambda a: a.reshape(b * t_len, a.shape[-1])
    x0, v_rows = _sgu_layer(flat(x), p["norm_mix"][0], p["sgu_w_in"][0], p["sgu_ln_g"][0],
                            p["sgu_ln_b"][0], p["sgu_w_s"][0], p["sgu_b_s"][0], p["sgu_w_out"][0],
                            chunk=sgu_chunk, tb=sgu_tb, write_v=write_v)
    ffn_w = (p["norm_ffn"][0], *(_Layer(w, 0) for w in p["ffn_bf16"]))
    proj_w = (p["norm_mix"][1], p["gdn_w_in"][0], p["gdn_w_conv"][0], p["gdn_a_log"][0], p["gdn_dt_bias"][0])
    if t_len > gdn_tb:
        x1, *proj = _ffn_proj(x0, conv_buf, *ffn_w, *proj_w, t_len=t_len, chunk=gdn_chunk, tb=gdn_tb)
    else:
        x1 = _ffn_layer(x0, *ffn_w, tb=tb)
        proj = _gdn_proj(x1, conv_buf, *proj_w, t_len=t_len, chunk=gdn_chunk, tb=gdn_tb)
    q, k, v, z, beta_t, g_t = (a.reshape(b, a.shape[0] // b, *a.shape[1:]) for a in proj[:-1])
    conv_tail = proj[-1]
    o, s_new = _gdn_core(q, k, v, beta_t, g_t, s0, p["gdn_w_onorm"][0], chunk=gdn_chunk, tb=core_tb)
    y = _gdn_tail(flat(o), flat(z), x1, p["gdn_w_out"][0], p["norm_ffn"][1],
                  *(_Layer(w, 1) for w in p["ffn_bf16"]), p["norm_final"], tb=tb)
    conv_new = conv_tail[:, SUBLANES - (CONV_K - 1):, :]
    if write_v:
        v_rows = v_rows.reshape(b, t_len, D_SGU)
    return y.reshape(b, t_len, D_MODEL), s_new, conv_new, v_rows


def kernel(x_prompt, x_sample, state_gdn, state_conv, norm_mix, norm_ffn, norm_final, sgu_w_in, sgu_ln_g, sgu_ln_b, sgu_w_s, sgu_b_s, sgu_w_out, gdn_w_in, gdn_w_conv, gdn_a_log, gdn_dt_bias, gdn_w_onorm, gdn_w_out, ffn_w_gate, ffn_w_up, ffn_w_down):
    p = dict(norm_mix=norm_mix, norm_ffn=norm_ffn, norm_final=norm_final, sgu_w_in=sgu_w_in,
             sgu_ln_g=sgu_ln_g, sgu_ln_b=sgu_ln_b, sgu_w_s=sgu_w_s, sgu_b_s=sgu_b_s,
             sgu_w_out=sgu_w_out, gdn_w_in=gdn_w_in, gdn_w_conv=gdn_w_conv, gdn_a_log=gdn_a_log,
             gdn_dt_bias=gdn_dt_bias, gdn_w_onorm=gdn_w_onorm, gdn_w_out=gdn_w_out,
             ffn_bf16=tuple(_mxu_weight(w) for w in (ffn_w_gate, ffn_w_up, ffn_w_down)))
    bp = x_prompt.shape[0]
    dec_seq = x_sample.shape[1]
    zero_conv = jnp.zeros((bp, CONV_K - 1, D_QKV), F32)
    zero_state = jnp.zeros((bp, V_HEADS, HEAD_DIM, HEAD_DIM), F32)
    yp, sp, cp, _ = _trunk(x_prompt, zero_conv, zero_state, p, sgu_chunk=SGU_CHUNK,
                           gdn_chunk=2 * GDN_CHUNK, tb=256, sgu_tb=512, gdn_tb=256, core_tb=512, write_v=False)
    ys, ss, cs, vs = _trunk(x_sample, state_conv[0], state_gdn[0], p, sgu_chunk=dec_seq,
                            gdn_chunk=dec_seq, tb=256, sgu_tb=256, gdn_tb=256, core_tb=dec_seq, write_v=True)
    return (yp, ys, sp[None], cp[None], ss[None], cs[None], vs[None])
```

```python
import functools
import math
from typing import NamedTuple

import jax
import jax.numpy as jnp
from jax import lax
from jax.experimental import pallas as pl
from jax.experimental.pallas import tpu as pltpu

D_MODEL = 1024
D_SGU = 2048
SGU_GROUPS = 8
SGU_GROUP_DIM = D_SGU // SGU_GROUPS
SGU_CHUNK = 128
HEAD_DIM = 128
K_HEADS = 8
V_HEADS = 16
D_QK = K_HEADS * HEAD_DIM
D_V = V_HEADS * HEAD_DIM
D_QKV = 2 * D_QK + D_V
CONV_K = 4
GDN_CHUNK = 64
D_FF = 2816
EPS = 1e-6
LN_EPS = 1e-5

VMEM_LIMIT_BYTES = 56 * 1024 * 1024
SUBLANES = 8
COL_BLOCK = 256
LANE_PAIR_TILE = 512
INV_BASE = 8
PROMPT_TILES = dict(tb=256, sgu_tb=512, gdn_tb=256, core_tb=512)
SAMPLE_TILES = dict(tb=256, sgu_tb=256, gdn_tb=256)

F32 = jnp.float32
BF16 = jnp.bfloat16
HIGHEST = lax.Precision.HIGHEST


class _Layer(NamedTuple):
    stacked: jax.Array
    index: int


def _const_spec(a):
    if isinstance(a, _Layer):
        index = a.index
        return pl.BlockSpec((None,) + a.stacked.shape[1:], lambda *_: (index, 0, 0),
                            pipeline_mode=pl.Buffered(1))
    nd = a.ndim
    return pl.BlockSpec(a.shape, lambda *_: (0,) * nd, pipeline_mode=pl.Buffered(1))


def _operands(consts):
    return [a.stacked if isinstance(a, _Layer) else a for a in consts]


def _mxu_weight(w):
    if isinstance(w, _Layer):
        return w
    wb = w.astype(BF16)
    if w.shape[-1] % LANE_PAIR_TILE == 0:
        wb = jnp.pad(wb, ((0, 0),) * (w.ndim - 1) + ((0, COL_BLOCK),))
    return wb


def _params(n_grid):
    return pltpu.CompilerParams(dimension_semantics=("arbitrary",) * n_grid,
                                vmem_limit_bytes=VMEM_LIMIT_BYTES)


def _rms_scale(x):
    return lax.rsqrt(jnp.mean(x * x, axis=-1, keepdims=True) + EPS)


def _silu(x):
    return x * jax.nn.sigmoid(x)


def _gelu_tanh(x):
    return 0.5 * x * (1.0 + jnp.tanh(math.sqrt(2.0 / math.pi) * (x + 0.044715 * (x * x * x))))


def _softplus(x):
    return jnp.maximum(x, 0.0) + jnp.log1p(jnp.exp(-jnp.abs(x)))


def _dot(a, b):
    return jnp.dot(a, b, preferred_element_type=F32)


def _dot_nt(a, b, precision=None):
    return lax.dot_general(a, b, (((1,), (1,)), ((), ())), precision=precision,
                           preferred_element_type=F32)


def _residual_dot_store(x_ref, a_ref, w_ref, o_ref):
    ssq = jnp.zeros((x_ref.shape[0], 1), F32)
    for c in range(0, x_ref.shape[1], COL_BLOCK):
        cs = slice(c, c + COL_BLOCK)
        y = x_ref[:, cs] + _dot(a_ref[...], w_ref[:, cs])
        o_ref[:, cs] = y
        ssq = ssq + jnp.sum(y * y, axis=-1, keepdims=True)
    return ssq


def _run(steps):
    for step in steps:
        step()


def _interleave(a_steps, b_steps):
    keyed = [((i + 0.5) / len(a_steps), 0, s) for i, s in enumerate(a_steps)]
    keyed += [((j + 0.5) / len(b_steps), 1, s) for j, s in enumerate(b_steps)]
    return [s for _, _, s in sorted(keyed, key=lambda t: t[:2])]


def _ffn_steps(x_ref, nw_ref, wg_ref, wu_ref, wd_ref, hb_ref, act_ref, o_refs, nf_ref=None):
    def norm():
        x = x_ref[...]
        hb_ref[...] = (x * _rms_scale(x) * nw_ref[...]).astype(BF16)

    def gate_up(c):
        g = _dot(hb_ref[...], wg_ref[:, c:c + COL_BLOCK])
        u = _dot(hb_ref[...], wu_ref[:, c:c + COL_BLOCK])
        act_ref[:, c:c + COL_BLOCK] = (_silu(g) * u).astype(BF16)

    ssq = [jnp.zeros((x_ref.shape[0], 1), F32)]

    def down(c):
        cs = slice(c, c + COL_BLOCK)
        y = x_ref[:, cs] + _dot(act_ref[...], wd_ref[:, cs])
        for o_ref in o_refs:
            o_ref[:, cs] = y
        if nf_ref is not None:
            ssq[0] = ssq[0] + jnp.sum(y * y, axis=-1, keepdims=True)

    def final_norm():
        for o_ref in o_refs:
            o_ref[...] = o_ref[...] * lax.rsqrt(ssq[0] * (1.0 / D_MODEL) + EPS) * nf_ref[...]

    steps = [norm] + [functools.partial(gate_up, c) for c in range(0, D_FF, COL_BLOCK)]
    steps += [functools.partial(down, c) for c in range(0, D_MODEL, COL_BLOCK)]
    return steps + ([final_norm] if nf_ref is not None else [])


def _sgu_kernel(x_ref, nw_ref, win_ref, lng_ref, lnb_ref, ws_ref, bs_ref, wout_ref,
                *rest, chunk, write_v):
    if write_v:
        o_ref, v_ref, hb_ref, u_ref, vv_ref, gated_ref = rest
    else:
        o_ref, hb_ref, u_ref, vv_ref, gated_ref = rest
        v_ref = None
    tb = x_ref.shape[0]
    x = x_ref[...]
    hb_ref[...] = (x * _rms_scale(x) * nw_ref[...]).astype(BF16)

    vsum = jnp.zeros((tb, 1), F32)
    for c in range(0, 2 * D_SGU, COL_BLOCK):
        uv = _gelu_tanh(_dot(hb_ref[...], win_ref[:, c:c + COL_BLOCK]))
        if c < D_SGU:
            u_ref[:, c:c + COL_BLOCK] = uv
        else:
            vv_ref[:, c - D_SGU:c - D_SGU + COL_BLOCK] = uv
            vsum = vsum + jnp.sum(uv, axis=-1, keepdims=True)
    mu = vsum * (1.0 / D_SGU)
    vvar = jnp.zeros((tb, 1), F32)
    for c in range(0, D_SGU, COL_BLOCK):
        d = vv_ref[:, c:c + COL_BLOCK] - mu
        vvar = vvar + jnp.sum(d * d, axis=-1, keepdims=True)
    rstd = lax.rsqrt(vvar * (1.0 / D_SGU) + LN_EPS)

    row = lax.broadcasted_iota(jnp.int32, (chunk, chunk), 0)
    col = lax.broadcasted_iota(jnp.int32, (chunk, chunk), 1)
    causal = row >= col
    for g in range(SGU_GROUPS):
        c0 = g * SGU_GROUP_DIM
        cs = slice(c0, c0 + SGU_GROUP_DIM)
        vn = (vv_ref[:, cs] - mu) * rstd * lng_ref[:, cs] + lnb_ref[:, cs]
        if write_v:
            v_ref[:, cs] = vn
        vnb = vn.astype(BF16)
        wsg = jnp.where(causal, ws_ref[g], 0.0).astype(BF16)
        bias = bs_ref[:, g:g + 1]
        for r in range(0, tb, chunk):
            mixed = _dot(wsg, vnb[r:r + chunk]) + bias
            gated_ref[r:r + chunk, cs] = (u_ref[r:r + chunk, cs] * mixed).astype(BF16)
    _residual_dot_store(x_ref, gated_ref, wout_ref, o_ref)


def _sgu_layer(x, norm_w, w_in, ln_g, ln_b, w_s, b_s, w_out, *, chunk, tb, write_v):
    n = x.shape[0]
    ws = w_s[:, :chunk, :chunk]
    bs_t = b_s[:, :chunk].T
    out_shape = [jax.ShapeDtypeStruct((n, D_MODEL), F32)]
    out_specs = [pl.BlockSpec((tb, D_MODEL), lambda i: (i, 0))]
    if write_v:
        out_shape.append(jax.ShapeDtypeStruct((n, D_SGU), F32))
        out_specs.append(pl.BlockSpec((tb, D_SGU), lambda i: (i, 0)))
    consts = (norm_w.reshape(1, D_MODEL), _mxu_weight(w_in), ln_g.reshape(1, D_SGU),
              ln_b.reshape(1, D_SGU), ws, bs_t, _mxu_weight(w_out))
    res = pl.pallas_call(
        functools.partial(_sgu_kernel, chunk=chunk, write_v=write_v),
        grid=(n // tb,),
        in_specs=[pl.BlockSpec((tb, D_MODEL), lambda i: (i, 0))] + [_const_spec(a) for a in consts],
        out_specs=out_specs,
        out_shape=out_shape,
        scratch_shapes=[
            pltpu.VMEM((tb, D_MODEL), BF16),
            pltpu.VMEM((tb, D_SGU), F32),
            pltpu.VMEM((tb, D_SGU), F32),
            pltpu.VMEM((tb, D_SGU), BF16),
        ],
        compiler_params=_params(1),
        name="sgu_mixer",
    )(x, *_operands(consts))
    return res if write_v else (res[0], None)


def _ffn_kernel(x_ref, nw_ref, wg_ref, wu_ref, wd_ref, o_ref, hb_ref, act_ref):
    _run(_ffn_steps(x_ref, nw_ref, wg_ref, wu_ref, wd_ref, hb_ref, act_ref, [o_ref]))


def _ffn_layer(x, norm_w, w_gate, w_up, w_down, *, tb):
    n = x.shape[0]
    consts = (norm_w.reshape(1, D_MODEL), _mxu_weight(w_gate), _mxu_weight(w_up), _mxu_weight(w_down))
    return pl.pallas_call(
        _ffn_kernel,
        grid=(n // tb,),
        in_specs=[pl.BlockSpec((tb, D_MODEL), lambda i: (i, 0))] + [_const_spec(a) for a in consts],
        out_specs=pl.BlockSpec((tb, D_MODEL), lambda i: (i, 0)),
        out_shape=jax.ShapeDtypeStruct((n, D_MODEL), F32),
        scratch_shapes=[pltpu.VMEM((tb, D_MODEL), BF16), pltpu.VMEM((tb, D_FF), BF16)],
        compiler_params=_params(1),
        name="ffn",
    )(x, *_operands(consts))


def _proj_steps(x_ref, hb_ref, hist_refs, carry, nw_ref, wqkv_ref, wz_ref, wbat_ref, wconv_ref, alog_ref,
                dtb_ref, q_ref, k_ref, v_ref, z_ref, beta_ref, g_ref, cnew_refs, *, chunk):
    tb = x_ref.shape[0]
    seg = tb // len(hist_refs)

    def norm():
        x = x_ref[...]
        hb_ref[...] = (x * _rms_scale(x) * nw_ref[...]).astype(BF16)

    def conv_block(c):
        cs = slice(c, c + COL_BLOCK)
        pre_all = _dot(hb_ref[...], wqkv_ref[:, cs])
        for s, (hist_ref, cnew_ref) in enumerate(zip(hist_refs, cnew_refs)):
            rs = slice(s * seg, (s + 1) * seg)
            pre = pre_all[rs]
            padded = jnp.concatenate([hist_ref[:, cs], pre], axis=0)
            if carry:
                hist_ref[:, cs] = pre[seg - SUBLANES:, :]
            cnew_ref[:, cs] = pre[seg - SUBLANES:, :]
            acc = padded * wconv_ref[0:1, cs]
            for i in range(1, CONV_K):
                acc = pltpu.roll(acc, 1, axis=0) + padded * wconv_ref[i:i + 1, cs]
            y = _silu(acc[SUBLANES:, :])
            if c < 2 * D_QK:
                dst, c_dst, scale = (q_ref, c, HEAD_DIM ** -0.5) if c < D_QK else (k_ref, c - D_QK, 1.0)
                for h in range(0, COL_BLOCK, HEAD_DIM):
                    yh = y[:, h:h + HEAD_DIM]
                    inv = lax.rsqrt(jnp.sum(yh * yh, axis=-1, keepdims=True) + EPS) * scale
                    dst[rs, c_dst + h:c_dst + h + HEAD_DIM] = (yh * inv).astype(BF16)
            else:
                v_ref[rs, c - 2 * D_QK:c - 2 * D_QK + COL_BLOCK] = y
        blk = c // COL_BLOCK
        if blk % 2 == 1:
            cz = (blk // 2) * COL_BLOCK
            z_ref[:, cz:cz + COL_BLOCK] = _dot(hb_ref[...], wz_ref[:, cz:cz + COL_BLOCK])

    def gates():
        ba_t = _dot_nt(wbat_ref[...], hb_ref[...])
        beta_t = jax.nn.sigmoid(ba_t[:V_HEADS])
        loga_t = -jnp.exp(alog_ref[...]) * _softplus(ba_t[V_HEADS:] + dtb_ref[...])
        row = lax.broadcasted_iota(jnp.int32, (chunk, chunk), 0)
        col = lax.broadcasted_iota(jnp.int32, (chunk, chunk), 1)
        upper = (row <= col).astype(F32)
        for j in range(tb // chunk):
            js = slice(j * chunk, (j + 1) * chunk)
            beta_ref[j] = beta_t[:, js]
            g_ref[j] = jnp.dot(loga_t[:, js], upper, precision=HIGHEST, preferred_element_type=F32)

    return [norm] + [functools.partial(conv_block, c) for c in range(0, D_QKV, COL_BLOCK)] + [gates]


def _gdn_proj_kernel(x_ref, cbuf_ref, nw_ref, wqkv_ref, wz_ref, wba_ref, wconv_ref,
                     alog_ref, dtb_ref,
                     q_ref, k_ref, v_ref, z_ref, beta_ref, g_ref, cnew_ref,
                     hb_ref, wbat_ref, *, chunk):
    @pl.when(pl.program_id(0) == 0)
    def _():
        wbat_ref[...] = wba_ref[...].T.astype(BF16)

    n_streams = cbuf_ref.shape[0]
    _run(_proj_steps(x_ref, hb_ref, [cbuf_ref.at[s] for s in range(n_streams)], False,
                     nw_ref, wqkv_ref, wz_ref, wbat_ref, wconv_ref, alog_ref, dtb_ref,
                     q_ref, k_ref, v_ref, z_ref, beta_ref, g_ref,
                     [cnew_ref.at[s] for s in range(n_streams)], chunk=chunk))


def _ffn_proj_kernel(x0_ref, cbuf_ref, nwf_ref, wg_ref, wu_ref, wd_ref,
                     nwp_ref, wqkv_ref, wz_ref, wba_ref, wconv_ref, alog_ref, dtb_ref,
                     x1_ref, q_ref, k_ref, v_ref, z_ref, beta_ref, g_ref, cnew_ref,
                     x1s_ref, hbf_ref, act_ref, hbp_ref, hist_ref, wbat_ref, *, chunk, tiles_per_stream):
    i = pl.program_id(0)

    @pl.when(i == 0)
    def _():
        x1s_ref[...] = jnp.zeros_like(x1s_ref)
        hist_ref[...] = jnp.zeros_like(hist_ref)
        wbat_ref[...] = wba_ref[...].T.astype(BF16)

    @pl.when((i >= 1) & ((i - 1) % tiles_per_stream == 0))
    def _():
        hist_ref[...] = cbuf_ref[0]

    ffn = _ffn_steps(x0_ref, nwf_ref, wg_ref, wu_ref, wd_ref, hbf_ref, act_ref, [x1_ref, x1s_ref])
    proj = _proj_steps(x1s_ref, hbp_ref, [hist_ref], True, nwp_ref, wqkv_ref, wz_ref, wbat_ref, wconv_ref,
                       alog_ref, dtb_ref, q_ref, k_ref, v_ref, z_ref, beta_ref, g_ref, [cnew_ref.at[0]],
                       chunk=chunk)
    proj[0]()
    _run(_interleave(ffn, proj[1:]))


def _gdn_proj(x, conv_buf, norm_w, w_in, w_conv, a_log, dt_bias, *, t_len, chunk, tb):
    n = x.shape[0]
    spt = tb // t_len
    npc = tb // chunk
    cbuf = jnp.pad(conv_buf, ((0, 0), (SUBLANES - (CONV_K - 1), 0), (0, 0)))
    consts = (norm_w.reshape(1, D_MODEL), _mxu_weight(w_in[:, :D_QKV]),
              _mxu_weight(w_in[:, D_QKV:D_QKV + D_V]), w_in[:, D_QKV + D_V:], w_conv,
              a_log.reshape(V_HEADS, 1), dt_bias.reshape(V_HEADS, 1))
    tok = lambda width: pl.BlockSpec((tb, width), lambda i: (i, 0))
    rows = pl.BlockSpec((npc, V_HEADS, chunk), lambda i: (i, 0, 0))
    conv_state = pl.BlockSpec((spt, SUBLANES, D_QKV), lambda i: (i, 0, 0))
    return pl.pallas_call(
        functools.partial(_gdn_proj_kernel, chunk=chunk),
        grid=(n // tb,),
        in_specs=[tok(D_MODEL), conv_state] + [_const_spec(a) for a in consts],
        out_specs=[tok(D_QK), tok(D_QK), tok(D_V), tok(D_V), rows, rows, conv_state],
        out_shape=[
            jax.ShapeDtypeStruct((n, D_QK), BF16),
            jax.ShapeDtypeStruct((n, D_QK), BF16),
            jax.ShapeDtypeStruct((n, D_V), F32),
            jax.ShapeDtypeStruct((n, D_V), F32),
            jax.ShapeDtypeStruct((n // chunk, V_HEADS, chunk), F32),
            jax.ShapeDtypeStruct((n // chunk, V_HEADS, chunk), F32),
            jax.ShapeDtypeStruct((n // t_len, SUBLANES, D_QKV), F32),
        ],
        scratch_shapes=[pltpu.VMEM((tb, D_MODEL), BF16), pltpu.VMEM((2 * V_HEADS, D_MODEL), BF16)],
        compiler_params=_params(1),
        name="gdn_proj",
    )(x, cbuf, *_operands(consts))


def _ffn_proj(x0, conv_buf, ffn_norm_w, w_gate, w_up, w_down, norm_w, w_in, w_conv, a_log, dt_bias,
              *, t_len, chunk, tb):
    n = x0.shape[0]
    n_tiles = n // tb
    tiles_per_stream = t_len // tb
    npc = tb // chunk
    cbuf = jnp.pad(conv_buf, ((0, 0), (SUBLANES - (CONV_K - 1), 0), (0, 0)))
    consts = (ffn_norm_w.reshape(1, D_MODEL), _mxu_weight(w_gate), _mxu_weight(w_up), _mxu_weight(w_down),
              norm_w.reshape(1, D_MODEL), _mxu_weight(w_in[:, :D_QKV]),
              _mxu_weight(w_in[:, D_QKV:D_QKV + D_V]), w_in[:, D_QKV + D_V:], w_conv,
              a_log.reshape(V_HEADS, 1), dt_bias.reshape(V_HEADS, 1))
    ffn_tile = lambda i: jnp.minimum(i, n_tiles - 1)
    proj_tile = lambda i: jnp.maximum(i - 1, 0)
    tok = lambda width: pl.BlockSpec((tb, width), lambda i: (proj_tile(i), 0))
    rows = pl.BlockSpec((npc, V_HEADS, chunk), lambda i: (proj_tile(i), 0, 0))
    conv_state = pl.BlockSpec((1, SUBLANES, D_QKV), lambda i: (proj_tile(i) // tiles_per_stream, 0, 0))
    return pl.pallas_call(
        functools.partial(_ffn_proj_kernel, chunk=chunk, tiles_per_stream=tiles_per_stream),
        grid=(n_tiles + 1,),
        in_specs=[pl.BlockSpec((tb, D_MODEL), lambda i: (ffn_tile(i), 0)), conv_state]
        + [_const_spec(a) for a in consts],
        out_specs=[pl.BlockSpec((tb, D_MODEL), lambda i: (ffn_tile(i), 0)),
                   tok(D_QK), tok(D_QK), tok(D_V), tok(D_V), rows, rows, conv_state],
        out_shape=[
            jax.ShapeDtypeStruct((n, D_MODEL), F32),
            jax.ShapeDtypeStruct((n, D_QK), BF16),
            jax.ShapeDtypeStruct((n, D_QK), BF16),
            jax.ShapeDtypeStruct((n, D_V), F32),
            jax.ShapeDtypeStruct((n, D_V), F32),
            jax.ShapeDtypeStruct((n // chunk, V_HEADS, chunk), F32),
            jax.ShapeDtypeStruct((n // chunk, V_HEADS, chunk), F32),
            jax.ShapeDtypeStruct((n // t_len, SUBLANES, D_QKV), F32),
        ],
        scratch_shapes=[pltpu.VMEM((tb, D_MODEL), F32), pltpu.VMEM((tb, D_MODEL), BF16),
                        pltpu.VMEM((tb, D_FF), BF16), pltpu.VMEM((tb, D_MODEL), BF16),
                        pltpu.VMEM((SUBLANES, D_QKV), F32), pltpu.VMEM((2 * V_HEADS, D_MODEL), BF16)],
        compiler_params=_params(1),
        name="ffn_gdn_proj",
    )(x0, cbuf, *_operands(consts))


def _gdn_core_kernel(q_ref, k_ref, v_ref, beta_ref, g_ref, s0_ref, won_ref, o_ref, s_ref,
                     t_ref, p_ref, x_ref, attn_ref, kdt_ref, col_ref, ks_ref, dl_ref, *, chunk):
    @pl.when(pl.program_id(1) == 0)
    def _():
        s_ref[...] = s0_ref[...]

    n_chunks = q_ref.shape[1] // chunk
    row = lax.broadcasted_iota(jnp.int32, (chunk, chunk), 0)
    col = lax.broadcasted_iota(jnp.int32, (chunk, chunk), 1)
    incl = row >= col
    strict = row > col
    eye = (row == col).astype(F32)
    diag_blk = (row // INV_BASE) == (col // INV_BASE)
    base_factors = INV_BASE.bit_length() - 1
    slot = lambda c, h: c * V_HEADS + h

    def kq_of(c, pr):
        rs = slice(c * chunk, (c + 1) * chunk)
        hs = slice(pr * HEAD_DIM, (pr + 1) * HEAD_DIM)
        return jnp.concatenate([k_ref[0, rs, hs], q_ref[0, rs, hs]], axis=0)

    for c in range(n_chunks):
        g_rows = g_ref[0, c]
        rows_t = jnp.concatenate([beta_ref[0, c], g_rows, jnp.exp(g_rows)], axis=0)
        col_ref[c] = rows_t.T

    def build_operands(c, pr):
        kq = kq_of(c, pr)
        kq_kt = _dot_nt(kq, kq[:chunk])
        k_t = kq[:chunk].astype(F32).T
        for h in (2 * pr, 2 * pr + 1):
            g_row = g_ref[0, c, h:h + 1, :]
            k_dec = jnp.exp(g_row[:, chunk - 1:chunk] - g_row)
            kdt_ref[slot(c, h)] = (k_t * k_dec).astype(BF16)
            b_col = col_ref[c, :, h:h + 1]
            g_col = col_ref[c, :, V_HEADS + h:V_HEADS + h + 1]
            decay = jnp.where(incl, jnp.exp(jnp.where(incl, g_col - g_row, 0.0)), 0.0)
            x_mat = jnp.where(strict, -(b_col * kq_kt[:chunk] * decay), 0.0)
            x_diag = jnp.where(diag_blk, x_mat, 0.0)
            x_ref[slot(c, h)] = x_mat.astype(BF16)
            t_ref[slot(c, h)] = eye + x_diag
            p_ref[slot(c, h)] = x_diag.astype(BF16)
            attn_ref[slot(c, h)] = jnp.where(incl, kq_kt[chunk:] * decay, 0.0).astype(BF16)

    def square_round(c):
        for h in range(V_HEADS):
            xd = p_ref[slot(c, h)]
            p_ref[slot(c, h)] = _dot(xd, xd).astype(BF16)

    def base_round(c, j):
        for h in range(V_HEADS):
            pb = p_ref[slot(c, h)]
            tm = t_ref[slot(c, h)]
            if j < base_factors - 1:
                pt = _dot(pb, jnp.concatenate([pb, tm.astype(BF16)], axis=1))
                p_ref[slot(c, h)] = pt[:, :chunk].astype(BF16)
                t_ref[slot(c, h)] = tm + pt[:, chunk:]
            else:
                t_ref[slot(c, h)] = tm + _dot(pb, tm.astype(BF16))

    def widen_round(c, width):
        off = ((row // (2 * width)) == (col // (2 * width))) & ((row // width) != (col // width))
        for h in range(V_HEADS):
            tb16 = t_ref[slot(c, h)].astype(BF16)
            w_mat = _dot(jnp.where(off, x_ref[slot(c, h)], jnp.zeros((), BF16)), tb16)
            t_new = t_ref[slot(c, h)] + _dot(tb16, w_mat.astype(BF16))
            if 2 * width < chunk:
                t_ref[slot(c, h)] = t_new
            else:
                p_ref[slot(c, h)] = t_new.astype(BF16)

    def inverse_rounds(c):
        rounds = [functools.partial(square_round, c)]
        rounds += [functools.partial(base_round, c, j) for j in range(1, base_factors)]
        width = INV_BASE
        while width < chunk:
            rounds.append(functools.partial(widen_round, c, width))
            width *= 2
        return rounds

    for c in range(n_chunks):
        for pr in range(K_HEADS):
            build_operands(c, pr)
    for same_round in zip(*[inverse_rounds(c) for c in range(n_chunks)]):
        _run(same_round)

    for c in range(n_chunks):
        rs = slice(c * chunk, (c + 1) * chunk)
        for pr in range(K_HEADS):
            s_pair = jnp.concatenate([s_ref[0, 2 * pr], s_ref[0, 2 * pr + 1]], axis=1).astype(BF16)
            ks_ref[pr] = _dot(kq_of(c, pr), s_pair)
        for h in range(V_HEADS):
            es = slice((h % 2) * HEAD_DIM, (h % 2 + 1) * HEAD_DIM)
            b_col = col_ref[c, :, h:h + 1]
            gam = col_ref[c, :, 2 * V_HEADS + h:2 * V_HEADS + h + 1]
            rhs = b_col * (v_ref[0, rs, h * HEAD_DIM:(h + 1) * HEAD_DIM] - gam * ks_ref[h // 2, :chunk, es])
            dl_ref[h] = _dot(p_ref[slot(c, h)], rhs.astype(BF16)).astype(BF16)
        for h in range(V_HEADS):
            s_decay = jnp.exp(g_ref[0, c, h:h + 1, chunk - 1:chunk])
            s_ref[0, h] = s_decay * s_ref[0, h] + _dot(kdt_ref[slot(c, h)], dl_ref[h])
        for h in range(V_HEADS):
            es = slice((h % 2) * HEAD_DIM, (h % 2 + 1) * HEAD_DIM)
            gam = col_ref[c, :, 2 * V_HEADS + h:2 * V_HEADS + h + 1]
            o = gam * ks_ref[h // 2, chunk:, es] + _dot(attn_ref[slot(c, h)], dl_ref[h])
            o_ref[0, rs, h * HEAD_DIM:(h + 1) * HEAD_DIM] = o * _rms_scale(o) * won_ref[...]


def _gdn_core(q, k, v, beta_t, g_t, s0, w_onorm, *, chunk, tb):
    b, t_len, _ = q.shape
    nt = t_len // tb
    npc = tb // chunk
    tok = lambda width: pl.BlockSpec((1, tb, width), lambda i, j: (i, j, 0))
    rows = pl.BlockSpec((1, npc, V_HEADS, chunk), lambda i, j: (i, j, 0, 0))
    state = pl.BlockSpec((1, V_HEADS, HEAD_DIM, HEAD_DIM), lambda i, j: (i, 0, 0, 0))
    return pl.pallas_call(
        functools.partial(_gdn_core_kernel, chunk=chunk),
        grid=(b, nt),
        in_specs=[tok(D_QK), tok(D_QK), tok(D_V), rows, rows, state, _const_spec(w_onorm.reshape(1, HEAD_DIM))],
        out_specs=[tok(D_V), state],
        out_shape=[jax.ShapeDtypeStruct((b, t_len, D_V), F32),
                   jax.ShapeDtypeStruct((b, V_HEADS, HEAD_DIM, HEAD_DIM), F32)],
        scratch_shapes=[pltpu.VMEM((npc * V_HEADS, chunk, chunk), F32),
                        pltpu.VMEM((npc * V_HEADS, chunk, chunk), BF16),
                        pltpu.VMEM((npc * V_HEADS, chunk, chunk), BF16),
                        pltpu.VMEM((npc * V_HEADS, chunk, chunk), BF16),
                        pltpu.VMEM((npc * V_HEADS, HEAD_DIM, chunk), BF16),
                        pltpu.VMEM((npc, chunk, 3 * V_HEADS), F32),
                        pltpu.VMEM((K_HEADS, 2 * chunk, 2 * HEAD_DIM), F32),
                        pltpu.VMEM((V_HEADS, chunk, HEAD_DIM), BF16)],
        compiler_params=_params(2),
        name="gdn_core",
    )(q, k, v, beta_t, g_t, s0, w_onorm.reshape(1, HEAD_DIM))


def _gdn_tail_kernel(o_ref, z_ref, x_ref, wout_ref, nw_ref, wg_ref, wu_ref, wd_ref, nf_ref,
                     y_ref, gated_ref, mid_ref, hb_ref, act_ref):
    for c in range(0, D_V, COL_BLOCK):
        cs = slice(c, c + COL_BLOCK)
        gated_ref[:, cs] = (o_ref[:, cs] * _silu(z_ref[:, cs])).astype(BF16)
    _residual_dot_store(x_ref, gated_ref, wout_ref, mid_ref)
    _run(_ffn_steps(mid_ref, nw_ref, wg_ref, wu_ref, wd_ref, hb_ref, act_ref, [y_ref], nf_ref))


def _gdn_tail(o, z, x, w_out, norm_w, w_gate, w_up, w_down, norm_final, *, tb):
    n = x.shape[0]
    consts = (_mxu_weight(w_out), norm_w.reshape(1, D_MODEL), _mxu_weight(w_gate), _mxu_weight(w_up),
              _mxu_weight(w_down), norm_final.reshape(1, D_MODEL))
    return pl.pallas_call(
        _gdn_tail_kernel,
        grid=(n // tb,),
        in_specs=[pl.BlockSpec((tb, D_V), lambda i: (i, 0)), pl.BlockSpec((tb, D_V), lambda i: (i, 0)),
                  pl.BlockSpec((tb, D_MODEL), lambda i: (i, 0))] + [_const_spec(a) for a in consts],
        out_specs=pl.BlockSpec((tb, D_MODEL), lambda i: (i, 0)),
        out_shape=jax.ShapeDtypeStruct((n, D_MODEL), F32),
        scratch_shapes=[pltpu.VMEM((tb, D_V), BF16), pltpu.VMEM((tb, D_MODEL), F32),
                        pltpu.VMEM((tb, D_MODEL), BF16), pltpu.VMEM((tb, D_FF), BF16)],
        compiler_params=_params(1),
        name="gdn_out_ffn",
    )(o, z, x, *_operands(consts))


def _trunk(x, conv_buf, s0, p, *, sgu_chunk, gdn_chunk, tb, sgu_tb, gdn_tb, core_tb, write_v):
    b, t_len, _ = x.shape
    flat = lambda a: a.reshape(b * t_len, a.shape[-1])
    x0, v_rows = _sgu_layer(flat(x), p["norm_mix"][0], p["sgu_w_in"][0], p["sgu_ln_g"][0],
                            p["sgu_ln_b"][0], p["sgu_w_s"][0], p["sgu_b_s"][0], p["sgu_w_out"][0],
                            chunk=sgu_chunk, tb=sgu_tb, write_v=write_v)
    ffn_w = (p["norm_ffn"][0], *(_Layer(w, 0) for w in p["ffn_bf16"]))
    proj_w = (p["norm_mix"][1], p["gdn_w_in"][0], p["gdn_w_conv"][0], p["gdn_a_log"][0], p["gdn_dt_bias"][0])
    if t_len > gdn_tb:
        x1, *proj = _ffn_proj(x0, conv_buf, *ffn_w, *proj_w, t_len=t_len, chunk=gdn_chunk, tb=gdn_tb)
    else:
        x1 = _ffn_layer(x0, *ffn_w, tb=tb)
        proj = _gdn_proj(x1, conv_buf, *proj_w, t_len=t_len, chunk=gdn_chunk, tb=gdn_tb)
    q, k, v, z, beta_t, g_t = (a.reshape(b, a.shape[0] // b, *a.shape[1:]) for a in proj[:-1])
    conv_tail = proj[-1]
    o, s_new = _gdn_core(q, k, v, beta_t, g_t, s0, p["gdn_w_onorm"][0], chunk=gdn_chunk, tb=core_tb)
    y = _gdn_tail(flat(o), flat(z), x1, p["gdn_w_out"][0], p["norm_ffn"][1],
                  *(_Layer(w, 1) for w in p["ffn_bf16"]), p["norm_final"], tb=tb)
    conv_new = conv_tail[:, SUBLANES - (CONV_K - 1):, :]
    if write_v:
        v_rows = v_rows.reshape(b, t_len, D_SGU)
    return y.reshape(b, t_len, D_MODEL), s_new, conv_new, v_rows


def kernel(x_prompt, x_sample, state_gdn, state_conv, norm_mix, norm_ffn, norm_final, sgu_w_in, sgu_ln_g, sgu_ln_b, sgu_w_s, sgu_b_s, sgu_w_out, gdn_w_in, gdn_w_conv, gdn_a_log, gdn_dt_bias, gdn_w_onorm, gdn_w_out, ffn_w_gate, ffn_w_up, ffn_w_down):
    p = dict(norm_mix=norm_mix, norm_ffn=norm_ffn, norm_final=norm_final, sgu_w_in=sgu_w_in,
             sgu_ln_g=sgu_ln_g, sgu_ln_b=sgu_ln_b, sgu_w_s=sgu_w_s, sgu_b_s=sgu_b_s,
             sgu_w_out=sgu_w_out, gdn_w_in=gdn_w_in, gdn_w_conv=gdn_w_conv, gdn_a_log=gdn_a_log,
             gdn_dt_bias=gdn_dt_bias, gdn_w_onorm=gdn_w_onorm, gdn_w_out=gdn_w_out,
             ffn_bf16=tuple(_mxu_weight(w) for w in (ffn_w_gate, ffn_w_up, ffn_w_down)))
    bp = x_prompt.shape[0]
    dec_seq = x_sample.shape[1]
    zero_conv = jnp.zeros((bp, CONV_K - 1, D_QKV), F32)
    zero_state = jnp.zeros((bp, V_HEADS, HEAD_DIM, HEAD_DIM), F32)
    yp, sp, cp, _ = _trunk(x_prompt, zero_conv, zero_state, p, sgu_chunk=SGU_CHUNK,
                           gdn_chunk=2 * GDN_CHUNK, write_v=False, **PROMPT_TILES)
    ys, ss, cs, vs = _trunk(x_sample, state_conv[0], state_gdn[0], p, sgu_chunk=dec_seq,
                            gdn_chunk=dec_seq, core_tb=dec_seq, write_v=True, **SAMPLE_TILES)
    return (yp, ys, sp[None], cp[None], ss[None], cs[None], vs[None])
```

```python
import functools
import math
from typing import NamedTuple

import jax
import jax.numpy as jnp
from jax import lax
from jax.experimental import pallas as pl
from jax.experimental.pallas import tpu as pltpu

D_MODEL = 1024
D_SGU = 2048
SGU_GROUPS = 8
SGU_GROUP_DIM = D_SGU // SGU_GROUPS
SGU_CHUNK = 128
HEAD_DIM = 128
K_HEADS = 8
V_HEADS = 16
D_QK = K_HEADS * HEAD_DIM
D_V = V_HEADS * HEAD_DIM
D_QKV = 2 * D_QK + D_V
CONV_K = 4
GDN_CHUNK = 64
D_FF = 2816
EPS = 1e-6
LN_EPS = 1e-5

VMEM_LIMIT_BYTES = 56 * 1024 * 1024
SUBLANES = 8
COL_BLOCK = 256
LANE_PAIR_TILE = 512
INV_BASE = 8
PROMPT_TILES = dict(tb=256, sgu_tb=512, gdn_tb=256, core_tb=512)
SAMPLE_TILES = dict(tb=256, sgu_tb=256, gdn_tb=256)

F32 = jnp.float32
BF16 = jnp.bfloat16
HIGHEST = lax.Precision.HIGHEST


class _Layer(NamedTuple):
    stacked: jax.Array
    index: int


def _const_spec(a):
    if isinstance(a, _Layer):
        index = a.index
        return pl.BlockSpec((None,) + a.stacked.shape[1:], lambda *_: (index, 0, 0),
                            pipeline_mode=pl.Buffered(1))
    nd = a.ndim
    return pl.BlockSpec(a.shape, lambda *_: (0,) * nd, pipeline_mode=pl.Buffered(1))


def _operands(consts):
    return [a.stacked if isinstance(a, _Layer) else a for a in consts]


def _mxu_weight(w):
    if isinstance(w, _Layer):
        return w
    wb = w.astype(BF16)
    if w.shape[-1] % LANE_PAIR_TILE == 0:
        wb = jnp.pad(wb, ((0, 0),) * (w.ndim - 1) + ((0, COL_BLOCK),))
    return wb


def _params(n_grid):
    return pltpu.CompilerParams(dimension_semantics=("arbitrary",) * n_grid,
                                vmem_limit_bytes=VMEM_LIMIT_BYTES)


def _rms_scale(x):
    return lax.rsqrt(jnp.mean(x * x, axis=-1, keepdims=True) + EPS)


NORM_ROWS = 64


def _rms_to_bf16(x_ref, nw_ref, hb_ref):
    for r in range(0, x_ref.shape[0], NORM_ROWS):
        x = x_ref[r:r + NORM_ROWS, :]
        hb_ref[r:r + NORM_ROWS, :] = (x * _rms_scale(x) * nw_ref[...]).astype(BF16)


def _silu(x):
    return x * jax.nn.sigmoid(x)


def _gelu_tanh(x):
    return 0.5 * x * (1.0 + jnp.tanh(math.sqrt(2.0 / math.pi) * (x + 0.044715 * (x * x * x))))


def _softplus(x):
    return jnp.maximum(x, 0.0) + jnp.log1p(jnp.exp(-jnp.abs(x)))


def _dot(a, b):
    return jnp.dot(a, b, preferred_element_type=F32)


def _dot_nt(a, b, precision=None):
    return lax.dot_general(a, b, (((1,), (1,)), ((), ())), precision=precision,
                           preferred_element_type=F32)


def _residual_dot_store(x_ref, a_ref, w_ref, o_ref):
    ssq = jnp.zeros((x_ref.shape[0], 1), F32)
    for c in range(0, x_ref.shape[1], COL_BLOCK):
        cs = slice(c, c + COL_BLOCK)
        y = x_ref[:, cs] + _dot(a_ref[...], w_ref[:, cs])
        o_ref[:, cs] = y
        ssq = ssq + jnp.sum(y * y, axis=-1, keepdims=True)
    return ssq


def _run(steps):
    for step in steps:
        step()


def _interleave(a_steps, b_steps):
    keyed = [((i + 0.5) / len(a_steps), 0, s) for i, s in enumerate(a_steps)]
    keyed += [((j + 0.5) / len(b_steps), 1, s) for j, s in enumerate(b_steps)]
    return [s for _, _, s in sorted(keyed, key=lambda t: t[:2])]


def _ffn_steps(x_ref, nw_ref, wg_ref, wu_ref, wd_ref, hb_ref, act_ref, o_refs, nf_ref=None):
    def norm():
        _rms_to_bf16(x_ref, nw_ref, hb_ref)

    def gate_up(c):
        g = _dot(hb_ref[...], wg_ref[:, c:c + COL_BLOCK])
        u = _dot(hb_ref[...], wu_ref[:, c:c + COL_BLOCK])
        act_ref[:, c:c + COL_BLOCK] = (_silu(g) * u).astype(BF16)

    ssq = [jnp.zeros((x_ref.shape[0], 1), F32)]

    def down(c):
        cs = slice(c, c + COL_BLOCK)
        y = x_ref[:, cs] + _dot(act_ref[...], wd_ref[:, cs])
        for o_ref in o_refs:
            o_ref[:, cs] = y
        if nf_ref is not None:
            ssq[0] = ssq[0] + jnp.sum(y * y, axis=-1, keepdims=True)

    def final_norm():
        for o_ref in o_refs:
            o_ref[...] = o_ref[...] * lax.rsqrt(ssq[0] * (1.0 / D_MODEL) + EPS) * nf_ref[...]

    steps = [norm] + [functools.partial(gate_up, c) for c in range(0, D_FF, COL_BLOCK)]
    steps += [functools.partial(down, c) for c in range(0, D_MODEL, COL_BLOCK)]
    return steps + ([final_norm] if nf_ref is not None else [])


def _sgu_kernel(x_ref, nw_ref, win_ref, lng_ref, lnb_ref, ws_ref, bs_ref, wout_ref,
                *rest, chunk, write_v):
    if write_v:
        o_ref, v_ref, hb_ref, u_ref, vv_ref, gated_ref = rest
    else:
        o_ref, hb_ref, u_ref, vv_ref, gated_ref = rest
        v_ref = None
    tb = x_ref.shape[0]
    _rms_to_bf16(x_ref, nw_ref, hb_ref)

    vsum = jnp.zeros((tb, 1), F32)
    for c in range(0, 2 * D_SGU, COL_BLOCK):
        uv = _gelu_tanh(_dot(hb_ref[...], win_ref[:, c:c + COL_BLOCK]))
        if c < D_SGU:
            u_ref[:, c:c + COL_BLOCK] = uv
        else:
            vv_ref[:, c - D_SGU:c - D_SGU + COL_BLOCK] = uv
            vsum = vsum + jnp.sum(uv, axis=-1, keepdims=True)
    mu = vsum * (1.0 / D_SGU)
    vvar = jnp.zeros((tb, 1), F32)
    for c in range(0, D_SGU, COL_BLOCK):
        d = vv_ref[:, c:c + COL_BLOCK] - mu
        vvar = vvar + jnp.sum(d * d, axis=-1, keepdims=True)
    rstd = lax.rsqrt(vvar * (1.0 / D_SGU) + LN_EPS)

    row = lax.broadcasted_iota(jnp.int32, (chunk, chunk), 0)
    col = lax.broadcasted_iota(jnp.int32, (chunk, chunk), 1)
    causal = row >= col
    for g in range(SGU_GROUPS):
        c0 = g * SGU_GROUP_DIM
        cs = slice(c0, c0 + SGU_GROUP_DIM)
        vn = (vv_ref[:, cs] - mu) * rstd * lng_ref[:, cs] + lnb_ref[:, cs]
        if write_v:
            v_ref[:, cs] = vn
        vnb = vn.astype(BF16)
        wsg = jnp.where(causal, ws_ref[g], 0.0).astype(BF16)
        bias = bs_ref[:, g:g + 1]
        for r in range(0, tb, chunk):
            mixed = _dot(wsg, vnb[r:r + chunk]) + bias
            gated_ref[r:r + chunk, cs] = (u_ref[r:r + chunk, cs] * mixed).astype(BF16)
    _residual_dot_store(x_ref, gated_ref, wout_ref, o_ref)


def _sgu_layer(x, norm_w, w_in, ln_g, ln_b, w_s, b_s, w_out, *, chunk, tb, write_v):
    n = x.shape[0]
    ws = w_s[:, :chunk, :chunk]
    bs_t = b_s[:, :chunk].T
    out_shape = [jax.ShapeDtypeStruct((n, D_MODEL), F32)]
    out_specs = [pl.BlockSpec((tb, D_MODEL), lambda i: (i, 0))]
    if write_v:
        out_shape.append(jax.ShapeDtypeStruct((n, D_SGU), F32))
        out_specs.append(pl.BlockSpec((tb, D_SGU), lambda i: (i, 0)))
    consts = (norm_w.reshape(1, D_MODEL), _mxu_weight(w_in), ln_g.reshape(1, D_SGU),
              ln_b.reshape(1, D_SGU), ws, bs_t, _mxu_weight(w_out))
    res = pl.pallas_call(
        functools.partial(_sgu_kernel, chunk=chunk, write_v=write_v),
        grid=(n // tb,),
        in_specs=[pl.BlockSpec((tb, D_MODEL), lambda i: (i, 0))] + [_const_spec(a) for a in consts],
        out_specs=out_specs,
        out_shape=out_shape,
        scratch_shapes=[
            pltpu.VMEM((tb, D_MODEL), BF16),
            pltpu.VMEM((tb, D_SGU), F32),
            pltpu.VMEM((tb, D_SGU), F32),
            pltpu.VMEM((tb, D_SGU), BF16),
        ],
        compiler_params=_params(1),
        name="sgu_mixer",
    )(x, *_operands(consts))
    return res if write_v else (res[0], None)


def _ffn_kernel(x_ref, nw_ref, wg_ref, wu_ref, wd_ref, o_ref, hb_ref, act_ref):
    _run(_ffn_steps(x_ref, nw_ref, wg_ref, wu_ref, wd_ref, hb_ref, act_ref, [o_ref]))


def _ffn_layer(x, norm_w, w_gate, w_up, w_down, *, tb):
    n = x.shape[0]
    consts = (norm_w.reshape(1, D_MODEL), _mxu_weight(w_gate), _mxu_weight(w_up), _mxu_weight(w_down))
    return pl.pallas_call(
        _ffn_kernel,
        grid=(n // tb,),
        in_specs=[pl.BlockSpec((tb, D_MODEL), lambda i: (i, 0))] + [_const_spec(a) for a in consts],
        out_specs=pl.BlockSpec((tb, D_MODEL), lambda i: (i, 0)),
        out_shape=jax.ShapeDtypeStruct((n, D_MODEL), F32),
        scratch_shapes=[pltpu.VMEM((tb, D_MODEL), BF16), pltpu.VMEM((tb, D_FF), BF16)],
        compiler_params=_params(1),
        name="ffn",
    )(x, *_operands(consts))


def _proj_steps(x_ref, hb_ref, hist_refs, carry, nw_ref, wqkv_ref, wz_ref, wbat_ref, wconv_ref, alog_ref,
                dtb_ref, q_ref, k_ref, v_ref, z_ref, beta_ref, g_ref, cnew_refs, *, chunk):
    tb = x_ref.shape[0]
    seg = tb // len(hist_refs)

    def norm():
        _rms_to_bf16(x_ref, nw_ref, hb_ref)

    def conv_block(c):
        cs = slice(c, c + COL_BLOCK)
        pre_all = _dot(hb_ref[...], wqkv_ref[:, cs])
        for s, (hist_ref, cnew_ref) in enumerate(zip(hist_refs, cnew_refs)):
            rs = slice(s * seg, (s + 1) * seg)
            pre = pre_all[rs]
            padded = jnp.concatenate([hist_ref[:, cs], pre], axis=0)
            if carry:
                hist_ref[:, cs] = pre[seg - SUBLANES:, :]
            cnew_ref[:, cs] = pre[seg - SUBLANES:, :]
            acc = padded * wconv_ref[0:1, cs]
            for i in range(1, CONV_K):
                acc = pltpu.roll(acc, 1, axis=0) + padded * wconv_ref[i:i + 1, cs]
            y = _silu(acc[SUBLANES:, :])
            if c < 2 * D_QK:
                dst, c_dst, scale = (q_ref, c, HEAD_DIM ** -0.5) if c < D_QK else (k_ref, c - D_QK, 1.0)
                for h in range(0, COL_BLOCK, HEAD_DIM):
                    yh = y[:, h:h + HEAD_DIM]
                    inv = lax.rsqrt(jnp.sum(yh * yh, axis=-1, keepdims=True) + EPS) * scale
                    dst[rs, c_dst + h:c_dst + h + HEAD_DIM] = (yh * inv).astype(BF16)
            else:
                v_ref[rs, c - 2 * D_QK:c - 2 * D_QK + COL_BLOCK] = y
        blk = c // COL_BLOCK
        if blk % 2 == 1:
            cz = (blk // 2) * COL_BLOCK
            z_ref[:, cz:cz + COL_BLOCK] = _dot(hb_ref[...], wz_ref[:, cz:cz + COL_BLOCK])

    def gates():
        ba_t = _dot_nt(wbat_ref[...], hb_ref[...])
        beta_t = jax.nn.sigmoid(ba_t[:V_HEADS])
        loga_t = -jnp.exp(alog_ref[...]) * _softplus(ba_t[V_HEADS:] + dtb_ref[...])
        row = lax.broadcasted_iota(jnp.int32, (chunk, chunk), 0)
        col = lax.broadcasted_iota(jnp.int32, (chunk, chunk), 1)
        upper = (row <= col).astype(F32)
        for j in range(tb // chunk):
            js = slice(j * chunk, (j + 1) * chunk)
            beta_ref[j] = beta_t[:, js]
            g_ref[j] = jnp.dot(loga_t[:, js], upper, precision=HIGHEST, preferred_element_type=F32)

    return [norm] + [functools.partial(conv_block, c) for c in range(0, D_QKV, COL_BLOCK)] + [gates]


def _gdn_proj_kernel(x_ref, cbuf_ref, nw_ref, wqkv_ref, wz_ref, wba_ref, wconv_ref,
                     alog_ref, dtb_ref,
                     q_ref, k_ref, v_ref, z_ref, beta_ref, g_ref, cnew_ref,
                     hb_ref, wbat_ref, *, chunk):
    @pl.when(pl.program_id(0) == 0)
    def _():
        wbat_ref[...] = wba_ref[...].T.astype(BF16)

    n_streams = cbuf_ref.shape[0]
    _run(_proj_steps(x_ref, hb_ref, [cbuf_ref.at[s] for s in range(n_streams)], False,
                     nw_ref, wqkv_ref, wz_ref, wbat_ref, wconv_ref, alog_ref, dtb_ref,
                     q_ref, k_ref, v_ref, z_ref, beta_ref, g_ref,
                     [cnew_ref.at[s] for s in range(n_streams)], chunk=chunk))


def _ffn_proj_kernel(x0_ref, cbuf_ref, nwf_ref, wg_ref, wu_ref, wd_ref,
                     nwp_ref, wqkv_ref, wz_ref, wba_ref, wconv_ref, alog_ref, dtb_ref,
                     x1_ref, q_ref, k_ref, v_ref, z_ref, beta_ref, g_ref, cnew_ref,
                     x1s_ref, hbf_ref, act_ref, hbp_ref, hist_ref, wbat_ref, *, chunk, tiles_per_stream):
    i = pl.program_id(0)

    @pl.when(i == 0)
    def _():
        x1s_ref[...] = jnp.zeros_like(x1s_ref)
        hist_ref[...] = jnp.zeros_like(hist_ref)
        wbat_ref[...] = wba_ref[...].T.astype(BF16)

    @pl.when((i >= 1) & ((i - 1) % tiles_per_stream == 0))
    def _():
        hist_ref[...] = cbuf_ref[0]

    ffn = _ffn_steps(x0_ref, nwf_ref, wg_ref, wu_ref, wd_ref, hbf_ref, act_ref, [x1_ref, x1s_ref])
    proj = _proj_steps(x1s_ref, hbp_ref, [hist_ref], True, nwp_ref, wqkv_ref, wz_ref, wbat_ref, wconv_ref,
                       alog_ref, dtb_ref, q_ref, k_ref, v_ref, z_ref, beta_ref, g_ref, [cnew_ref.at[0]],
                       chunk=chunk)
    proj[0]()
    _run(_interleave(ffn, proj[1:]))


def _gdn_proj(x, conv_buf, norm_w, w_in, w_conv, a_log, dt_bias, *, t_len, chunk, tb):
    n = x.shape[0]
    spt = tb // t_len
    npc = tb // chunk
    cbuf = jnp.pad(conv_buf, ((0, 0), (SUBLANES - (CONV_K - 1), 0), (0, 0)))
    consts = (norm_w.reshape(1, D_MODEL), _mxu_weight(w_in[:, :D_QKV]),
              _mxu_weight(w_in[:, D_QKV:D_QKV + D_V]), w_in[:, D_QKV + D_V:], w_conv,
              a_log.reshape(V_HEADS, 1), dt_bias.reshape(V_HEADS, 1))
    tok = lambda width: pl.BlockSpec((tb, width), lambda i: (i, 0))
    rows = pl.BlockSpec((npc, V_HEADS, chunk), lambda i: (i, 0, 0))
    conv_state = pl.BlockSpec((spt, SUBLANES, D_QKV), lambda i: (i, 0, 0))
    return pl.pallas_call(
        functools.partial(_gdn_proj_kernel, chunk=chunk),
        grid=(n // tb,),
        in_specs=[tok(D_MODEL), conv_state] + [_const_spec(a) for a in consts],
        out_specs=[tok(D_QK), tok(D_QK), tok(D_V), tok(D_V), rows, rows, conv_state],
        out_shape=[
            jax.ShapeDtypeStruct((n, D_QK), BF16),
            jax.ShapeDtypeStruct((n, D_QK), BF16),
            jax.ShapeDtypeStruct((n, D_V), F32),
            jax.ShapeDtypeStruct((n, D_V), F32),
            jax.ShapeDtypeStruct((n // chunk, V_HEADS, chunk), F32),
            jax.ShapeDtypeStruct((n // chunk, V_HEADS, chunk), F32),
            jax.ShapeDtypeStruct((n // t_len, SUBLANES, D_QKV), F32),
        ],
        scratch_shapes=[pltpu.VMEM((tb, D_MODEL), BF16), pltpu.VMEM((2 * V_HEADS, D_MODEL), BF16)],
        compiler_params=_params(1),
        name="gdn_proj",
    )(x, cbuf, *_operands(consts))


def _ffn_proj(x0, conv_buf, ffn_norm_w, w_gate, w_up, w_down, norm_w, w_in, w_conv, a_log, dt_bias,
              *, t_len, chunk, tb):
    n = x0.shape[0]
    n_tiles = n // tb
    tiles_per_stream = t_len // tb
    npc = tb // chunk
    cbuf = jnp.pad(conv_buf, ((0, 0), (SUBLANES - (CONV_K - 1), 0), (0, 0)))
    consts = (ffn_norm_w.reshape(1, D_MODEL), _mxu_weight(w_gate), _mxu_weight(w_up), _mxu_weight(w_down),
              norm_w.reshape(1, D_MODEL), _mxu_weight(w_in[:, :D_QKV]),
              _mxu_weight(w_in[:, D_QKV:D_QKV + D_V]), w_in[:, D_QKV + D_V:], w_conv,
              a_log.reshape(V_HEADS, 1), dt_bias.reshape(V_HEADS, 1))
    ffn_tile = lambda i: jnp.minimum(i, n_tiles - 1)
    proj_tile = lambda i: jnp.maximum(i - 1, 0)
    tok = lambda width: pl.BlockSpec((tb, width), lambda i: (proj_tile(i), 0))
    rows = pl.BlockSpec((npc, V_HEADS, chunk), lambda i: (proj_tile(i), 0, 0))
    conv_state = pl.BlockSpec((1, SUBLANES, D_QKV), lambda i: (proj_tile(i) // tiles_per_stream, 0, 0))
    return pl.pallas_call(
        functools.partial(_ffn_proj_kernel, chunk=chunk, tiles_per_stream=tiles_per_stream),
        grid=(n_tiles + 1,),
        in_specs=[pl.BlockSpec((tb, D_MODEL), lambda i: (ffn_tile(i), 0)), conv_state]
        + [_const_spec(a) for a in consts],
        out_specs=[pl.BlockSpec((tb, D_MODEL), lambda i: (ffn_tile(i), 0)),
                   tok(D_QK), tok(D_QK), tok(D_V), tok(D_V), rows, rows, conv_state],
        out_shape=[
            jax.ShapeDtypeStruct((n, D_MODEL), F32),
            jax.ShapeDtypeStruct((n, D_QK), BF16),
            jax.ShapeDtypeStruct((n, D_QK), BF16),
            jax.ShapeDtypeStruct((n, D_V), F32),
            jax.ShapeDtypeStruct((n, D_V), F32),
            jax.ShapeDtypeStruct((n // chunk, V_HEADS, chunk), F32),
            jax.ShapeDtypeStruct((n // chunk, V_HEADS, chunk), F32),
            jax.ShapeDtypeStruct((n // t_len, SUBLANES, D_QKV), F32),
        ],
        scratch_shapes=[pltpu.VMEM((tb, D_MODEL), F32), pltpu.VMEM((tb, D_MODEL), BF16),
                        pltpu.VMEM((tb, D_FF), BF16), pltpu.VMEM((tb, D_MODEL), BF16),
                        pltpu.VMEM((SUBLANES, D_QKV), F32), pltpu.VMEM((2 * V_HEADS, D_MODEL), BF16)],
        compiler_params=_params(1),
        name="ffn_gdn_proj",
    )(x0, cbuf, *_operands(consts))


def _gdn_core_kernel(q_ref, k_ref, v_ref, beta_ref, g_ref, s0_ref, won_ref, o_ref, s_ref,
                     t_ref, p_ref, x_ref, attn_ref, kdt_ref, col_ref, ks_ref, dl_ref, *, chunk):
    @pl.when(pl.program_id(1) == 0)
    def _():
        s_ref[...] = s0_ref[...]

    n_chunks = q_ref.shape[1] // chunk
    row = lax.broadcasted_iota(jnp.int32, (chunk, chunk), 0)
    col = lax.broadcasted_iota(jnp.int32, (chunk, chunk), 1)
    incl = row >= col
    strict = row > col
    eye = (row == col).astype(F32)
    diag_blk = (row // INV_BASE) == (col // INV_BASE)
    base_factors = INV_BASE.bit_length() - 1
    slot = lambda c, h: c * V_HEADS + h

    def kq_of(c, pr):
        rs = slice(c * chunk, (c + 1) * chunk)
        hs = slice(pr * HEAD_DIM, (pr + 1) * HEAD_DIM)
        return jnp.concatenate([k_ref[0, rs, hs], q_ref[0, rs, hs]], axis=0)

    for c in range(n_chunks):
        g_rows = g_ref[0, c]
        rows_t = jnp.concatenate([beta_ref[0, c], g_rows, jnp.exp(g_rows)], axis=0)
        col_ref[c] = rows_t.T

    def build_operands(c, pr):
        kq = kq_of(c, pr)
        kq_kt = _dot_nt(kq, kq[:chunk])
        k_t = kq[:chunk].astype(F32).T
        for h in (2 * pr, 2 * pr + 1):
            g_row = g_ref[0, c, h:h + 1, :]
            k_dec = jnp.exp(g_row[:, chunk - 1:chunk] - g_row)
            kdt_ref[slot(c, h)] = (k_t * k_dec).astype(BF16)
            b_col = col_ref[c, :, h:h + 1]
            g_col = col_ref[c, :, V_HEADS + h:V_HEADS + h + 1]
            decay = jnp.where(incl, jnp.exp(jnp.where(incl, g_col - g_row, 0.0)), 0.0)
            x_mat = jnp.where(strict, -(b_col * kq_kt[:chunk] * decay), 0.0)
            x_diag = jnp.where(diag_blk, x_mat, 0.0)
            x_ref[slot(c, h)] = x_mat.astype(BF16)
            t_ref[slot(c, h)] = eye + x_diag
            p_ref[slot(c, h)] = x_diag.astype(BF16)
            attn_ref[slot(c, h)] = jnp.where(incl, kq_kt[chunk:] * decay, 0.0).astype(BF16)

    def square_round(c):
        for h in range(V_HEADS):
            xd = p_ref[slot(c, h)]
            p_ref[slot(c, h)] = _dot(xd, xd).astype(BF16)

    def base_round(c, j):
        for h in range(V_HEADS):
            pb = p_ref[slot(c, h)]
            tm = t_ref[slot(c, h)]
            if j < base_factors - 1:
                pt = _dot(pb, jnp.concatenate([pb, tm.astype(BF16)], axis=1))
                p_ref[slot(c, h)] = pt[:, :chunk].astype(BF16)
                t_ref[slot(c, h)] = tm + pt[:, chunk:]
            else:
                t_ref[slot(c, h)] = tm + _dot(pb, tm.astype(BF16))

    def widen_round(c, width):
        off = ((row // (2 * width)) == (col // (2 * width))) & ((row // width) != (col // width))
        for h in range(V_HEADS):
            tb16 = t_ref[slot(c, h)].astype(BF16)
            w_mat = _dot(jnp.where(off, x_ref[slot(c, h)], jnp.zeros((), BF16)), tb16)
            t_new = t_ref[slot(c, h)] + _dot(tb16, w_mat.astype(BF16))
            if 2 * width < chunk:
                t_ref[slot(c, h)] = t_new
            else:
                p_ref[slot(c, h)] = t_new.astype(BF16)

    def inverse_rounds(c):
        rounds = [functools.partial(square_round, c)]
        rounds += [functools.partial(base_round, c, j) for j in range(1, base_factors)]
        width = INV_BASE
        while width < chunk:
            rounds.append(functools.partial(widen_round, c, width))
            width *= 2
        return rounds

    for c in range(n_chunks):
        for pr in range(K_HEADS):
            build_operands(c, pr)
    for same_round in zip(*[inverse_rounds(c) for c in range(n_chunks)]):
        _run(same_round)

    for c in range(n_chunks):
        rs = slice(c * chunk, (c + 1) * chunk)
        for pr in range(K_HEADS):
            s_pair = jnp.concatenate([s_ref[0, 2 * pr], s_ref[0, 2 * pr + 1]], axis=1).astype(BF16)
            ks_ref[pr] = _dot(kq_of(c, pr), s_pair)
        for h in range(V_HEADS):
            es = slice((h % 2) * HEAD_DIM, (h % 2 + 1) * HEAD_DIM)
            b_col = col_ref[c, :, h:h + 1]
            gam = col_ref[c, :, 2 * V_HEADS + h:2 * V_HEADS + h + 1]
            rhs = b_col * (v_ref[0, rs, h * HEAD_DIM:(h + 1) * HEAD_DIM] - gam * ks_ref[h // 2, :chunk, es])
            dl_ref[h] = _dot(p_ref[slot(c, h)], rhs.astype(BF16)).astype(BF16)
        for h in range(V_HEADS):
            s_decay = jnp.exp(g_ref[0, c, h:h + 1, chunk - 1:chunk])
            s_ref[0, h] = s_decay * s_ref[0, h] + _dot(kdt_ref[slot(c, h)], dl_ref[h])
        for h in range(V_HEADS):
            es = slice((h % 2) * HEAD_DIM, (h % 2 + 1) * HEAD_DIM)
            gam = col_ref[c, :, 2 * V_HEADS + h:2 * V_HEADS + h + 1]
            o = gam * ks_ref[h // 2, chunk:, es] + _dot(attn_ref[slot(c, h)], dl_ref[h])
            o_ref[0, rs, h * HEAD_DIM:(h + 1) * HEAD_DIM] = o * _rms_scale(o) * won_ref[...]


def _gdn_core(q, k, v, beta_t, g_t, s0, w_onorm, *, chunk, tb):
    b, t_len, _ = q.shape
    nt = t_len // tb
    npc = tb // chunk
    tok = lambda width: pl.BlockSpec((1, tb, width), lambda i, j: (i, j, 0))
    rows = pl.BlockSpec((1, npc, V_HEADS, chunk), lambda i, j: (i, j, 0, 0))
    state = pl.BlockSpec((1, V_HEADS, HEAD_DIM, HEAD_DIM), lambda i, j: (i, 0, 0, 0))
    return pl.pallas_call(
        functools.partial(_gdn_core_kernel, chunk=chunk),
        grid=(b, nt),
        in_specs=[tok(D_QK), tok(D_QK), tok(D_V), rows, rows, state, _const_spec(w_onorm.reshape(1, HEAD_DIM))],
        out_specs=[tok(D_V), state],
        out_shape=[jax.ShapeDtypeStruct((b, t_len, D_V), F32),
                   jax.ShapeDtypeStruct((b, V_HEADS, HEAD_DIM, HEAD_DIM), F32)],
        scratch_shapes=[pltpu.VMEM((npc * V_HEADS, chunk, chunk), F32),
                        pltpu.VMEM((npc * V_HEADS, chunk, chunk), BF16),
                        pltpu.VMEM((npc * V_HEADS, chunk, chunk), BF16),
                        pltpu.VMEM((npc * V_HEADS, chunk, chunk), BF16),
                        pltpu.VMEM((npc * V_HEADS, HEAD_DIM, chunk), BF16),
                        pltpu.VMEM((npc, chunk, 3 * V_HEADS), F32),
                        pltpu.VMEM((K_HEADS, 2 * chunk, 2 * HEAD_DIM), F32),
                        pltpu.VMEM((V_HEADS, chunk, HEAD_DIM), BF16)],
        compiler_params=_params(2),
        name="gdn_core",
    )(q, k, v, beta_t, g_t, s0, w_onorm.reshape(1, HEAD_DIM))


def _gdn_tail_kernel(o_ref, z_ref, x_ref, wout_ref, nw_ref, wg_ref, wu_ref, wd_ref, nf_ref,
                     y_ref, gated_ref, mid_ref, hb_ref, act_ref):
    for c in range(0, D_V, COL_BLOCK):
        cs = slice(c, c + COL_BLOCK)
        gated_ref[:, cs] = (o_ref[:, cs] * _silu(z_ref[:, cs])).astype(BF16)
    _residual_dot_store(x_ref, gated_ref, wout_ref, mid_ref)
    _run(_ffn_steps(mid_ref, nw_ref, wg_ref, wu_ref, wd_ref, hb_ref, act_ref, [y_ref], nf_ref))


def _gdn_tail(o, z, x, w_out, norm_w, w_gate, w_up, w_down, norm_final, *, tb):
    n = x.shape[0]
    consts = (_mxu_weight(w_out), norm_w.reshape(1, D_MODEL), _mxu_weight(w_gate), _mxu_weight(w_up),
              _mxu_weight(w_down), norm_final.reshape(1, D_MODEL))
    return pl.pallas_call(
        _gdn_tail_kernel,
        grid=(n // tb,),
        in_specs=[pl.BlockSpec((tb, D_V), lambda i: (i, 0)), pl.BlockSpec((tb, D_V), lambda i: (i, 0)),
                  pl.BlockSpec((tb, D_MODEL), lambda i: (i, 0))] + [_const_spec(a) for a in consts],
        out_specs=pl.BlockSpec((tb, D_MODEL), lambda i: (i, 0)),
        out_shape=jax.ShapeDtypeStruct((n, D_MODEL), F32),
        scratch_shapes=[pltpu.VMEM((tb, D_V), BF16), pltpu.VMEM((tb, D_MODEL), F32),
                        pltpu.VMEM((tb, D_MODEL), BF16), pltpu.VMEM((tb, D_FF), BF16)],
        compiler_params=_params(1),
        name="gdn_out_ffn",
    )(o, z, x, *_operands(consts))


def _trunk(x, conv_buf, s0, p, *, sgu_chunk, gdn_chunk, tb, sgu_tb, gdn_tb, core_tb, write_v):
    b, t_len, _ = x.shape
    flat = lambda a: a.reshape(b * t_len, a.shape[-1])
    x0, v_rows = _sgu_layer(flat(x), p["norm_mix"][0], p["sgu_w_in"][0], p["sgu_ln_g"][0],
                            p["sgu_ln_b"][0], p["sgu_w_s"][0], p["sgu_b_s"][0], p["sgu_w_out"][0],
                            chunk=sgu_chunk, tb=sgu_tb, write_v=write_v)
    ffn_w = (p["norm_ffn"][0], *(_Layer(w, 0) for w in p["ffn_bf16"]))
    proj_w = (p["norm_mix"][1], p["gdn_w_in"][0], p["gdn_w_conv"][0], p["gdn_a_log"][0], p["gdn_dt_bias"][0])
    if t_len > gdn_tb:
        x1, *proj = _ffn_proj(x0, conv_buf, *ffn_w, *proj_w, t_len=t_len, chunk=gdn_chunk, tb=gdn_tb)
    else:
        x1 = _ffn_layer(x0, *ffn_w, tb=tb)
        proj = _gdn_proj(x1, conv_buf, *proj_w, t_len=t_len, chunk=gdn_chunk, tb=gdn_tb)
    q, k, v, z, beta_t, g_t = (a.reshape(b, a.shape[0] // b, *a.shape[1:]) for a in proj[:-1])
    conv_tail = proj[-1]
    o, s_new = _gdn_core(q, k, v, beta_t, g_t, s0, p["gdn_w_onorm"][0], chunk=gdn_chunk, tb=core_tb)
    y = _gdn_tail(flat(o), flat(z), x1, p["gdn_w_out"][0], p["norm_ffn"][1],
                  *(_Layer(w, 1) for w in p["ffn_bf16"]), p["norm_final"], tb=tb)
    conv_new = conv_tail[:, SUBLANES - (CONV_K - 1):, :]
    if write_v:
        v_rows = v_rows.reshape(b, t_len, D_SGU)
    return y.reshape(b, t_len, D_MODEL), s_new, conv_new, v_rows


def kernel(x_prompt, x_sample, state_gdn, state_conv, norm_mix, norm_ffn, norm_final, sgu_w_in, sgu_ln_g, sgu_ln_b, sgu_w_s, sgu_b_s, sgu_w_out, gdn_w_in, gdn_w_conv, gdn_a_log, gdn_dt_bias, gdn_w_onorm, gdn_w_out, ffn_w_gate, ffn_w_up, ffn_w_down):
    p = dict(norm_mix=norm_mix, norm_ffn=norm_ffn, norm_final=norm_final, sgu_w_in=sgu_w_in,
             sgu_ln_g=sgu_ln_g, sgu_ln_b=sgu_ln_b, sgu_w_s=sgu_w_s, sgu_b_s=sgu_b_s,
             sgu_w_out=sgu_w_out, gdn_w_in=gdn_w_in, gdn_w_conv=gdn_w_conv, gdn_a_log=gdn_a_log,
             gdn_dt_bias=gdn_dt_bias, gdn_w_onorm=gdn_w_onorm, gdn_w_out=gdn_w_out,
             ffn_bf16=tuple(_mxu_weight(w) for w in (ffn_w_gate, ffn_w_up, ffn_w_down)))
    bp = x_prompt.shape[0]
    dec_seq = x_sample.shape[1]
    zero_conv = jnp.zeros((bp, CONV_K - 1, D_QKV), F32)
    zero_state = jnp.zeros((bp, V_HEADS, HEAD_DIM, HEAD_DIM), F32)
    yp, sp, cp, _ = _trunk(x_prompt, zero_conv, zero_state, p, sgu_chunk=SGU_CHUNK,
                           gdn_chunk=2 * GDN_CHUNK, write_v=False, **PROMPT_TILES)
    ys, ss, cs, vs = _trunk(x_sample, state_conv[0], state_gdn[0], p, sgu_chunk=dec_seq,
                            gdn_chunk=dec_seq, core_tb=dec_seq, write_v=True, **SAMPLE_TILES)
    return (yp, ys, sp[None], cp[None], ss[None], cs[None], vs[None])
```
